```python
import functools
import jax, jax.numpy as jnp
from jax import lax
import numpy as np

D_MODEL = 1024
BATCH = 4
SEQ = 4096
DEPTH = 1
DEC_BATCH = 32
DEC_SEQ = 4
PAST_LEN = 8192
PAGE_SIZE = 128

N_HEADS = 8
HEAD_DIM = 64
N_KV_HEADS = 4
KV_GROUPS = N_HEADS // N_KV_HEADS
ATT_WIDTH = N_HEADS * HEAD_DIM
KV_WIDTH = N_KV_HEADS * HEAD_DIM
ROPE_DIM = HEAD_DIM // 4
ROPE_THETA = 500000.0
IDX_HEADS = 8
IDX_DIM = 64
IDX_W_SCALE = (IDX_HEADS * IDX_DIM) ** -0.5
TOPK_MAX = 256
Q_BLOCK = 128
CONV_WIDTH = D_MODEL // 2
CONV_K = 3
N_EXPERTS = 32
TOP_K = 4
D_FF = D_MODEL
SWIGLU_LIMIT = 7.0
SWIGLU_ALPHA = 1.702
MOE_BLOCK = 128
LN_EPS = 1e-5
DN_ALPHA = (2 * DEPTH) ** 0.25
DN_BETA = (8 * DEPTH) ** -0.25
IN_SIZES = (ATT_WIDTH, KV_WIDTH, KV_WIDTH, IDX_HEADS * IDX_DIM, IDX_DIM, IDX_HEADS,
            CONV_WIDTH, CONV_WIDTH, CONV_WIDTH, D_MODEL, D_MODEL)
IN_WIDTH = ATT_WIDTH + 2 * KV_WIDTH + IDX_HEADS * IDX_DIM + IDX_DIM + IDX_HEADS + 3 * CONV_WIDTH + 2 * D_MODEL

kernel_name = 'dsa_shortconv_moe_hybrid_step'


def layer_norm(x, g, b):
    xf = x.astype(jnp.float32)
    mu = xf.mean(-1, keepdims=True)
    var = jnp.square(xf - mu).mean(-1, keepdims=True)
    return ((xf - mu) * lax.rsqrt(var + LN_EPS) * g.astype(jnp.float32) + b.astype(jnp.float32)).astype(x.dtype)


def partial_rope(x, pos):
    half = ROPE_DIM // 2
    inv_freq = ROPE_THETA ** (-jnp.arange(half, dtype=jnp.float32) * 2.0 / ROPE_DIM)
    ang = pos[:, None] * inv_freq[None, :]
    cos = jnp.cos(ang)[None, :, None, :].astype(x.dtype)
    sin = jnp.sin(ang)[None, :, None, :].astype(x.dtype)
    x1 = x[..., :half]
    x2 = x[..., half:ROPE_DIM]
    return jnp.concatenate([x1 * cos - x2 * sin, x2 * cos + x1 * sin, x[..., ROPE_DIM:]], axis=-1)


def gather_rows(arr, idx):
    return jax.vmap(lambda a, i: a[i])(arr, idx)


def index_select(qi, ki, wi, allowed, topk):
    dots = jnp.einsum('bqhd,bsd->bqhs', qi.astype(jnp.float32), ki.astype(jnp.float32))
    score = jnp.einsum('bqh,bqhs->bqs', wi.astype(jnp.float32) * IDX_W_SCALE, jax.nn.relu(dots))
    score = jnp.where(allowed, score, -jnp.inf)
    vals, sel = lax.top_k(score, topk)
    return sel, jnp.isfinite(vals)


def sparse_attend(q, ks, vs, valid):
    b, t = q.shape[:2]
    qg = q.reshape(b, t, N_KV_HEADS, KV_GROUPS, HEAD_DIM)
    logits = jnp.einsum('btgrd,btkgd->btgrk', qg.astype(jnp.float32), ks.astype(jnp.float32)) * HEAD_DIM ** -0.5
    logits = jnp.where(valid[:, :, None, None, :], logits, -jnp.inf)
    p = jax.nn.softmax(logits, axis=-1)
    o = jnp.einsum('btgrk,btkgd->btgrd', p.astype(vs.dtype), vs)
    return o.reshape(b, t, ATT_WIDTH)


def attend_prompt(q, k, v, qi, ki, wi):
    b, s = q.shape[:2]
    topk = min(TOPK_MAX, s // 4)
    key_pos = jnp.arange(s)

    def block(i):
        s0 = i * Q_BLOCK
        qb = lax.dynamic_slice_in_dim(q, s0, Q_BLOCK, axis=1)
        qib = lax.dynamic_slice_in_dim(qi, s0, Q_BLOCK, axis=1)
        wib = lax.dynamic_slice_in_dim(wi, s0, Q_BLOCK, axis=1)
        allowed = key_pos[None, :] <= (s0 + jnp.arange(Q_BLOCK))[:, None]
        sel, valid = index_select(qib, ki, wib, allowed, topk)
        return sparse_attend(qb, gather_rows(k, sel), gather_rows(v, sel), valid)

    out = lax.map(block, jnp.arange(s // Q_BLOCK))
    return out.transpose(1, 0, 2, 3).reshape(b, s, ATT_WIDTH)


def attend_sample(q, k, v, qi, ki, wi, cache_k, cache_v, cache_ki, page_table):
    db, t = q.shape[:2]
    n_pages = page_table.shape[1]
    past = n_pages * PAGE_SIZE
    ki_all = jnp.concatenate([cache_ki[page_table].reshape(db, past, IDX_DIM), ki], axis=1)
    n_keys = past + t
    topk = min(TOPK_MAX, n_keys // 4)
    allowed = jnp.arange(n_keys)[None, :] <= (past + jnp.arange(t))[:, None]
    sel, valid = index_select(qi, ki_all, wi, allowed, topk)
    is_new = (sel >= past)[..., None, None]
    past_i = jnp.minimum(sel, past - 1)
    phys = jnp.take_along_axis(page_table, (past_i // PAGE_SIZE).reshape(db, -1), axis=1).reshape(sel.shape)
    slot = past_i % PAGE_SIZE
    new_i = jnp.clip(sel - past, 0, t - 1)
    ks = jnp.where(is_new, gather_rows(k, new_i), cache_k[phys, slot])
    vs = jnp.where(is_new, gather_rows(v, new_i), cache_v[phys, slot])
    return sparse_attend(q, ks, vs, valid)


def short_conv(xin, bg, cg, conv_w, prev):
    u = cg * xin
    ext = jnp.concatenate([prev.astype(u.dtype), u], axis=1)
    t = u.shape[1]
    y = conv_w[0] * ext[:, 0:t]
    for j in range(1, CONV_K):
        y = y + conv_w[j] * ext[:, j:j + t]
    return bg * y, ext[:, -(CONV_K - 1):]


def moe_ffn(h, w_router, b_router, w_gate_up, b_gate_up, w_down, b_down):
    n, d = h.shape
    logits = (h @ w_router + b_router).astype(jnp.float32)
    top_logit, top_e = lax.top_k(logits, TOP_K)
    comb = jax.nn.softmax(top_logit, axis=-1)
    nk = n * TOP_K
    n_blocks = -(-(nk + N_EXPERTS * (MOE_BLOCK - 1)) // MOE_BLOCK)
    flat_e = top_e.reshape(nk)
    order = jnp.argsort(flat_e).astype(jnp.int32)
    sorted_e = flat_e[order]
    counts = jnp.zeros((N_EXPERTS,), jnp.int32).at[flat_e].add(1)
    start = jnp.cumsum(counts) - counts
    padded = (counts + MOE_BLOCK - 1) // MOE_BLOCK * MOE_BLOCK
    pad_end = jnp.cumsum(padded)
    pad_start = pad_end - padded
    dest = pad_start[sorted_e] + jnp.arange(nk, dtype=jnp.int32) - start[sorted_e]
    rows = n_blocks * MOE_BLOCK
    row_src = jnp.zeros((rows,), jnp.int32).at[dest].set(order)
    row_valid = jnp.zeros((rows,), bool).at[dest].set(True)
    block_e = jnp.minimum(jnp.searchsorted(pad_end, jnp.arange(n_blocks, dtype=jnp.int32) * MOE_BLOCK, side='right'), N_EXPERTS - 1)
    row_tok = row_src // TOP_K
    xb = h[row_tok].reshape(n_blocks, MOE_BLOCK, d)

    def expert_block(args):
        xblk, e = args
        gu = xblk @ w_gate_up[e] + b_gate_up[e]
        gate, up = jnp.split(gu, 2, axis=-1)
        gate = jnp.minimum(gate, SWIGLU_LIMIT)
        up = jnp.clip(up, -SWIGLU_LIMIT, SWIGLU_LIMIT)
        act = gate * jax.nn.sigmoid(SWIGLU_ALPHA * gate) * (up + 1)
        return act @ w_down[e] + b_down[e]

    y = lax.map(expert_block, (xb, block_e)).reshape(rows, d)
    wts = jnp.where(row_valid, comb.reshape(nk)[row_src], 0.0).astype(h.dtype)
    return jnp.zeros_like(h).at[row_tok].add(y * wts[:, None])


def decoder_layer(x, c, p, pos, attend, conv_prev):
    b, t, _ = x.shape
    sh1, sc1, g1, sh2, sc2, g2 = jnp.split((c @ p['w_ada'] + p['b_ada'])[:, None, :], 6, axis=-1)
    h = x * (1 + sc1) + sh1
    cuts = [int(v) for v in np.cumsum(IN_SIZES)[:-1]]
    q, k, v, qi, ki, wi, xin, bg, cg, ga, gb = jnp.split(h @ p['w_in'], cuts, axis=-1)
    q = partial_rope(q.reshape(b, t, N_HEADS, HEAD_DIM), pos)
    k = partial_rope(k.reshape(b, t, N_KV_HEADS, HEAD_DIM), pos)
    v = v.reshape(b, t, N_KV_HEADS, HEAD_DIM)
    qi = partial_rope(qi.reshape(b, t, IDX_HEADS, IDX_DIM), pos)
    ki = partial_rope(ki[:, :, None, :], pos)[:, :, 0, :]
    att = attend(q, k, v, qi, ki, wi)
    conv, conv_state = short_conv(xin, bg, cg, p['conv_w'], conv_prev)
    mix = (jax.nn.sigmoid(ga) * (att @ p['w_branch_a']) + jax.nn.sigmoid(gb) * (conv @ p['w_branch_b'])) @ p['w_o']
    x = layer_norm(DN_ALPHA * x + g1 * mix, p['ln1_g'], p['ln1_b'])
    h2 = x * (1 + sc2) + sh2
    ffn = moe_ffn(h2.reshape(b * t, D_MODEL), p['w_router'], p['b_router'], p['w_gate_up'],
                  p['b_gate_up'], p['w_down'], p['b_down']).reshape(b, t, D_MODEL)
    x = layer_norm(DN_ALPHA * x + g2 * ffn, p['ln2_g'], p['ln2_b'])
    return x, k, v, ki, conv_state


def setup_inputs(seed: int = 0) -> dict:
    key = jax.random.key(seed)
    keys = iter(jax.random.split(key, 32))

    def nrm(shape, scale=1.0):
        return jax.random.normal(next(keys), shape, jnp.float32) * scale

    n_pages = PAST_LEN // PAGE_SIZE
    n_pool = (5 * DEC_BATCH * n_pages + 3) // 4
    page_table = jax.random.permutation(next(keys), n_pool)[:DEC_BATCH * n_pages].reshape(DEC_BATCH, n_pages).astype(jnp.int32)
    return {
        'x_prompt': nrm((BATCH, SEQ, D_MODEL)),
        'x_sample': nrm((DEC_BATCH, DEC_SEQ, D_MODEL)),
        'cache_k': nrm((DEPTH, n_pool, PAGE_SIZE, N_KV_HEADS, HEAD_DIM)),
        'cache_v': nrm((DEPTH, n_pool, PAGE_SIZE, N_KV_HEADS, HEAD_DIM)),
        'cache_idx_k': nrm((DEPTH, n_pool, PAGE_SIZE, IDX_DIM)),
        'state_conv': nrm((DEPTH, DEC_BATCH, CONV_K - 1, CONV_WIDTH)),
        'page_table': page_table,
        'c_prompt': nrm((BATCH, D_MODEL)),
        'c_sample': nrm((DEC_BATCH, D_MODEL)),
        'w_ada': nrm((DEPTH, D_MODEL, 6 * D_MODEL), 0.3 * D_MODEL ** -0.5),
        'b_ada': nrm((DEPTH, 6 * D_MODEL), 0.01),
        'w_in': nrm((DEPTH, D_MODEL, IN_WIDTH), D_MODEL ** -0.5),
        'conv_w': nrm((DEPTH, CONV_K, CONV_WIDTH), CONV_K ** -0.5),
        'w_branch_a': nrm((DEPTH, ATT_WIDTH, D_MODEL), DN_BETA * ATT_WIDTH ** -0.5),
        'w_branch_b': nrm((DEPTH, CONV_WIDTH, D_MODEL), DN_BETA * CONV_WIDTH ** -0.5),
        'w_o': nrm((DEPTH, D_MODEL, D_MODEL), DN_BETA * D_MODEL ** -0.5),
        'ln1_g': 1.0 + nrm((DEPTH, D_MODEL), 0.01),
        'ln1_b': nrm((DEPTH, D_MODEL), 0.01),
        'ln2_g': 1.0 + nrm((DEPTH, D_MODEL), 0.01),
        'ln2_b': nrm((DEPTH, D_MODEL), 0.01),
        'w_router': nrm((DEPTH, D_MODEL, N_EXPERTS), D_MODEL ** -0.5),
        'b_router': nrm((DEPTH, N_EXPERTS), 0.01),
        'w_gate_up': nrm((DEPTH, N_EXPERTS, D_MODEL, 2 * D_FF), D_MODEL ** -0.5),
        'b_gate_up': nrm((DEPTH, N_EXPERTS, 2 * D_FF), 0.01),
        'w_down': nrm((DEPTH, N_EXPERTS, D_FF, D_MODEL), DN_BETA * D_FF ** -0.5),
        'b_down': nrm((DEPTH, N_EXPERTS, D_MODEL), 0.01),
    }


def reference(x_prompt, x_sample, cache_k, cache_v, cache_idx_k, state_conv, page_table, c_prompt, c_sample,
              w_ada, b_ada, w_in, conv_w, w_branch_a, w_branch_b, w_o, ln1_g, ln1_b, ln2_g, ln2_b,
              w_router, b_router, w_gate_up, b_gate_up, w_down, b_down):
    pos_p = jnp.arange(x_prompt.shape[1], dtype=jnp.float32)
    pos_s = PAST_LEN + jnp.arange(x_sample.shape[1], dtype=jnp.float32)
    xp, xs = x_prompt, x_sample
    kp_l, vp_l, kip_l, cp_l, ks_l, vs_l, kis_l, cs_l = [], [], [], [], [], [], [], []
    for l in range(DEPTH):
        p = {'w_ada': w_ada[l], 'b_ada': b_ada[l], 'w_in': w_in[l], 'conv_w': conv_w[l],
             'w_branch_a': w_branch_a[l], 'w_branch_b': w_branch_b[l], 'w_o': w_o[l],
             'ln1_g': ln1_g[l], 'ln1_b': ln1_b[l], 'ln2_g': ln2_g[l], 'ln2_b': ln2_b[l],
             'w_router': w_router[l], 'b_router': b_router[l], 'w_gate_up': w_gate_up[l],
             'b_gate_up': b_gate_up[l], 'w_down': w_down[l], 'b_down': b_down[l]}
        conv0 = jnp.zeros((xp.shape[0], CONV_K - 1, CONV_WIDTH), xp.dtype)
        xp, kp, vp, kip, cp = decoder_layer(xp, c_prompt, p, pos_p, attend_prompt, conv0)
        attend_s = functools.partial(attend_sample, cache_k=cache_k[l], cache_v=cache_v[l],
                                     cache_ki=cache_idx_k[l], page_table=page_table)
        xs, ks, vs, kis, cs = decoder_layer(xs, c_sample, p, pos_s, attend_s, state_conv[l])
        kp_l.append(kp); vp_l.append(vp); kip_l.append(kip); cp_l.append(cp)
        ks_l.append(ks); vs_l.append(vs); kis_l.append(kis); cs_l.append(cs)
    new_k_prompt = jnp.stack(kp_l)
    new_v_prompt = jnp.stack(vp_l)
    new_idx_k_prompt = jnp.stack(kip_l)
    new_conv_prompt = jnp.stack(cp_l)
    new_k_sample = jnp.stack(ks_l)
    new_v_sample = jnp.stack(vs_l)
    new_idx_k_sample = jnp.stack(kis_l)
    new_conv_sample = jnp.stack(cs_l)
    return (xp, xs, new_k_prompt, new_v_prompt, new_idx_k_prompt, new_conv_prompt,
            new_k_sample, new_v_sample, new_idx_k_sample, new_conv_sample)
```

```python
import functools

import jax
import jax.numpy as jnp
import numpy as np
from jax import lax
from jax.experimental import pallas as pl
from jax.experimental.pallas import tpu as pltpu

f32 = jnp.float32
bf16 = jnp.bfloat16
i32 = jnp.int32

N_HEADS = 8
HEAD_DIM = 64
N_KV_HEADS = 4
KV_GROUPS = N_HEADS // N_KV_HEADS
ATT_WIDTH = N_HEADS * HEAD_DIM
KV_WIDTH = N_KV_HEADS * HEAD_DIM
ROPE_DIM = HEAD_DIM // 4
ROPE_HALF = ROPE_DIM // 2
ROPE_THETA = 500000.0
IDX_HEADS = 8
IDX_DIM = 64
IDX_W_SCALE = (IDX_HEADS * IDX_DIM) ** -0.5
TOPK_MAX = 256
Q_BLOCK = 128
PAGE_SIZE = 128
CONV_K = 3
N_EXPERTS = 32
TOP_K = 4
SWIGLU_LIMIT = 7.0
SWIGLU_ALPHA = 1.702
LN_EPS = 1e-5

LANES = 128
SUBLANES = 8
VMEM_LIMIT = 56 * 1024 * 1024

ROW_TILE = 256
MOE_ROWS = 256
COMBINE_TILE = 128
KEY_CLASS = 1024
INT_MIN = int(np.iinfo(np.int32).min)
NEG_BIG = -1e30


def _cparams(sem):
    return pltpu.CompilerParams(dimension_semantics=sem, vmem_limit_bytes=VMEM_LIMIT)


def _mm(a, b):
    return jnp.dot(a.astype(bf16), b.astype(bf16), preferred_element_type=f32)


def _mm_nt(a, b):
    return lax.dot_general(a.astype(bf16), b.astype(bf16), (((1,), (1,)), ((), ())), preferred_element_type=f32)


def _layer_norm(y, g, b):
    mu = jnp.mean(y, axis=-1, keepdims=True)
    d = y - mu
    var = jnp.mean(d * d, axis=-1, keepdims=True)
    return d * lax.rsqrt(var + LN_EPS) * g + b


def _ada_kernel(c_ref, w_ref, b_ref, o_ref):
    o_ref[...] = _mm(c_ref[...], w_ref[...]) + b_ref[...]


def _ada(c_all, w_ada, b_ada):
    n, d = c_all.shape
    width = w_ada.shape[1]
    return pl.pallas_call(
        _ada_kernel,
        grid=(width // d,),
        in_specs=[pl.BlockSpec((n, d), lambda j: (0, 0)),
                  pl.BlockSpec((d, d), lambda j: (0, j)),
                  pl.BlockSpec((1, d), lambda j: (0, j))],
        out_specs=pl.BlockSpec((n, d), lambda j: (0, j)),
        out_shape=jax.ShapeDtypeStruct((n, width), f32),
        compiler_params=_cparams(("arbitrary",)),
        name="ada",
    )(c_all, w_ada, b_ada.reshape(1, width))


_SEG = {}
_off = 0
for _name, _w in (("q", ATT_WIDTH), ("k", KV_WIDTH), ("v", KV_WIDTH), ("qi", IDX_HEADS * IDX_DIM),
                  ("ki", LANES), ("wi", LANES), ("xin", 512), ("bg", 512), ("cg", 512)):
    _SEG[_name] = (_off, _off + _w)
    _off += _w
PROJ_WIDTH = _off


def _rope(t, c, s1, s2):
    w = t.shape[1]
    reps = w // LANES
    if reps > 1:
        c, s1, s2 = (jnp.tile(a, (1, reps)) for a in (c, s1, s2))
    return t * c + pltpu.roll(t, w - ROPE_HALF, 1) * s1 + pltpu.roll(t, ROPE_HALF, 1) * s2


def _project(x_ref, sc_ref, sh_ref, cos_ref, s1_ref, s2_ref, w_ref):
    h = x_ref[...] * (1.0 + sc_ref[...]) + sh_ref[...]
    hb = h.astype(bf16)
    c, s1, s2 = cos_ref[...], s1_ref[...], s2_ref[...]

    def seg(name):
        a, b = _SEG[name]
        return jnp.dot(hb, w_ref[:, a:b], preferred_element_type=f32)

    q = _rope(seg("q"), c, s1, s2)
    k = _rope(seg("k"), c, s1, s2)
    v = seg("v")
    qi = _rope(seg("qi"), c, s1, s2)
    ki = _rope(seg("ki"), c, s1, s2)[:, :IDX_DIM]
    wi = seg("wi")
    u = seg("cg") * seg("xin")
    return q, k, v, qi, ki, wi, u, seg("bg")


def _conv_out(u, bg, cw_ref, sh1, sh2):
    y = cw_ref[0:1, :] * sh2 + cw_ref[1:2, :] * sh1
    y = y + cw_ref[2:3, :] * u
    return bg * y


def _inproj_prompt_kernel(x_ref, sc_ref, sh_ref, cos_ref, s1_ref, s2_ref, w_ref, cw_ref,
                          q_ref, k_ref, v_ref, qi_ref, ki_ref, wi_ref, conv_ref, cst_ref, carry_ref):
    @pl.when(pl.program_id(1) == 0)
    def _():
        carry_ref[...] = jnp.zeros_like(carry_ref)

    q, k, v, qi, ki, wi, u, bg = _project(x_ref, sc_ref, sh_ref, cos_ref, s1_ref, s2_ref, w_ref)
    q_ref[...], k_ref[...], v_ref[...], qi_ref[...], ki_ref[...], wi_ref[...] = q, k, v, qi, ki, wi
    tm = u.shape[0]
    rows = lax.broadcasted_iota(i32, u.shape, 0)
    c0, c1 = carry_ref[0:1, :], carry_ref[1:2, :]
    sh1 = jnp.where(rows == 0, c1, pltpu.roll(u, 1, 0))
    sh2 = jnp.where(rows == 0, c0, jnp.where(rows == 1, c1, pltpu.roll(u, 2, 0)))
    conv_ref[...] = _conv_out(u, bg, cw_ref, sh1, sh2)
    tail = u[tm - 2:tm, :]
    carry_ref[0:2, :] = tail
    cst_ref[...] = tail


def _inproj_sample_kernel(x_ref, sc_ref, sh_ref, cos_ref, s1_ref, s2_ref, w_ref, cw_ref, p1_ref, p2_ref,
                          q_ref, k_ref, v_ref, qi_ref, ki_ref, wi_ref, conv_ref, u_ref, *, dec_seq):
    q, k, v, qi, ki, wi, u, bg = _project(x_ref, sc_ref, sh_ref, cos_ref, s1_ref, s2_ref, w_ref)
    q_ref[...], k_ref[...], v_ref[...], qi_ref[...], ki_ref[...], wi_ref[...] = q, k, v, qi, ki, wi
    t = lax.broadcasted_iota(i32, u.shape, 0) % dec_seq
    sh1 = jnp.where(t == 0, p1_ref[...], pltpu.roll(u, 1, 0))
    sh2 = jnp.where(t < 2, p2_ref[...], pltpu.roll(u, 2, 0))
    conv_ref[...] = _conv_out(u, bg, cw_ref, sh1, sh2)
    u_ref[...] = u


def _proj_out_shapes(lead, d_model):
    del d_model
    widths = (ATT_WIDTH, KV_WIDTH, KV_WIDTH, IDX_HEADS * IDX_DIM, IDX_DIM, LANES, 512)
    return [jax.ShapeDtypeStruct(lead + (w,), f32) for w in widths]


def _inproj_prompt(x, sc1, sh1, tabs, w_proj, conv_w):
    b, s, d = x.shape
    tm = min(ROW_TILE, s)
    widths = (ATT_WIDTH, KV_WIDTH, KV_WIDTH, IDX_HEADS * IDX_DIM, IDX_DIM, LANES, 512)
    row = lambda w: pl.BlockSpec((None, tm, w), lambda bi, i: (bi, i, 0))
    per_b = pl.BlockSpec((None, 1, d), lambda bi, i: (bi, 0, 0))
    tab = pl.BlockSpec((tm, LANES), lambda bi, i: (i, 0))
    return pl.pallas_call(
        _inproj_prompt_kernel,
        grid=(b, s // tm),
        in_specs=[row(d), per_b, per_b, tab, tab, tab,
                  pl.BlockSpec((d, PROJ_WIDTH), lambda bi, i: (0, 0)),
                  pl.BlockSpec((CONV_K, 512), lambda bi, i: (0, 0))],
        out_specs=[row(w) for w in widths] + [pl.BlockSpec((None, CONV_K - 1, 512), lambda bi, i: (bi, 0, 0))],
        out_shape=_proj_out_shapes((b, s), d) + [jax.ShapeDtypeStruct((b, CONV_K - 1, 512), f32)],
        scratch_shapes=[pltpu.VMEM((SUBLANES, 512), f32)],
        compiler_params=_cparams(("arbitrary", "arbitrary")),
        name="inproj_prompt",
    )(x, sc1, sh1, *tabs, w_proj, conv_w)


def _inproj_sample(x, sc1, sh1, tabs, w_proj, conv_w, p1, p2, dec_seq):
    n, d = x.shape
    widths = (ATT_WIDTH, KV_WIDTH, KV_WIDTH, IDX_HEADS * IDX_DIM, IDX_DIM, LANES, 512, 512)
    full = lambda r, w: pl.BlockSpec((r, w), lambda i: (0, 0))
    return pl.pallas_call(
        functools.partial(_inproj_sample_kernel, dec_seq=dec_seq),
        grid=(1,),
        in_specs=[full(n, d), full(n, d), full(n, d), full(n, LANES), full(n, LANES), full(n, LANES),
                  full(d, PROJ_WIDTH), full(CONV_K, 512), full(n, 512), full(n, 512)],
        out_specs=[full(n, w) for w in widths],
        out_shape=[jax.ShapeDtypeStruct((n, w), f32) for w in widths],
        compiler_params=_cparams(("arbitrary",)),
        name="inproj_sample",
    )(x, sc1, sh1, *tabs, w_proj, conv_w, p1, p2)


def _sort_key(score):
    bits = pltpu.bitcast(score + 0.0, i32)
    return bits ^ ((bits >> 31) & jnp.int32(0x7FFFFFFF))


def _select_topk(key_ref, n_cols, topk):
    n_rows = key_ref.shape[0]
    kf = float(topk)

    def count(pred):
        return jnp.sum(jnp.where(pred, 1.0, 0.0), axis=1, keepdims=True)

    t0 = jnp.full((n_rows, 1), INT_MIN, i32)
    zero = jnp.zeros((n_rows, 1), i32)
    t = jnp.where(count(key_ref[:, 0:n_cols] >= zero) >= kf, zero, t0)

    def step(i, t):
        cand = t | jnp.left_shift(jnp.int32(1), 30 - i)
        return jnp.where(count(key_ref[:, 0:n_cols] >= cand) >= kf, cand, t)

    t = lax.fori_loop(0, 31, step, t)
    keys = key_ref[:, 0:n_cols]
    n_ge = count(keys >= t)
    col = lax.broadcasted_iota(i32, (n_rows, n_cols), 1)
    n_bits = int(n_cols).bit_length()

    def tie_search(_):
        need = kf - count(key_ref[:, 0:n_cols] > t)

        def tstep(i, j):
            cand = j + jnp.left_shift(jnp.int32(1), n_bits - 1 - i)
            below = count((key_ref[:, 0:n_cols] == t) & (col < cand))
            return jnp.where(below < need, cand, j)

        return lax.fori_loop(0, n_bits, tstep, jnp.zeros((n_rows, 1), i32))

    jmax = lax.cond(jnp.max(n_ge) > kf, tie_search, lambda _: jnp.full((n_rows, 1), n_cols, i32), 0)
    return (keys > INT_MIN) & ((keys > t) | ((keys == t) & (col <= jmax)))


def _masked_attention(q, k, v, sel):
    logits = jnp.where(sel, _mm_nt(q, k), -jnp.inf)
    m = jnp.max(logits, axis=1, keepdims=True)
    p = jnp.exp(logits - m)
    den = jnp.sum(p, axis=1, keepdims=True)
    return _mm(p, v) / den


def _attn_prompt_body(qi_ref, wi_ref, q_ref, ki_ref, k_ref, v_ref, o_ref, key_ref, s0, n_keys, topk):
    qb = qi_ref.shape[0]
    ki = ki_ref[0:n_keys, :]
    wi = wi_ref[...] * IDX_W_SCALE
    score = jnp.zeros((qb, n_keys), f32)
    for h in range(IDX_HEADS):
        d = _mm_nt(qi_ref[:, h * IDX_DIM:(h + 1) * IDX_DIM], ki)
        score = score + wi[:, h:h + 1] * jnp.maximum(d, 0.0)
    col = lax.broadcasted_iota(i32, (qb, n_keys), 1)
    qpos = s0 + lax.broadcasted_iota(i32, (qb, n_keys), 0)
    key_ref[:, 0:n_keys] = jnp.where(col <= qpos, _sort_key(score), INT_MIN)
    sel = _select_topk(key_ref, n_keys, topk)
    scale = HEAD_DIM ** -0.5
    for h in range(N_HEADS):
        g = h // KV_GROUPS
        o = _masked_attention(q_ref[:, h * HEAD_DIM:(h + 1) * HEAD_DIM] * scale,
                              k_ref[0:n_keys, g * HEAD_DIM:(g + 1) * HEAD_DIM],
                              v_ref[0:n_keys, g * HEAD_DIM:(g + 1) * HEAD_DIM], sel)
        o_ref[:, h * HEAD_DIM:(h + 1) * HEAD_DIM] = o


def _attn_prompt_kernel(qi_ref, wi_ref, q_ref, ki_ref, k_ref, v_ref, o_ref, key_ref, *, key_class, topk):
    s0 = pl.program_id(1) * Q_BLOCK
    n_classes = ki_ref.shape[0] // key_class
    cls = (s0 + Q_BLOCK - 1) // key_class
    for c in range(n_classes):
        @pl.when(cls == c)
        def _(c=c):
            _attn_prompt_body(qi_ref, wi_ref, q_ref, ki_ref, k_ref, v_ref, o_ref, key_ref,
                              s0, (c + 1) * key_class, topk)


def _attn_prompt(qi, wi, q, ki, k, v):
    b, s, _ = q.shape
    key_class = min(KEY_CLASS, s)
    topk = min(TOPK_MAX, s // 4)
    blk = lambda w: pl.BlockSpec((None, Q_BLOCK, w), lambda bi, i: (bi, i, 0))
    whole = lambda w: pl.BlockSpec((None, s, w), lambda bi, i: (bi, 0, 0))
    return pl.pallas_call(
        functools.partial(_attn_prompt_kernel, key_class=key_class, topk=topk),
        grid=(b, s // Q_BLOCK),
        in_specs=[blk(IDX_HEADS * IDX_DIM), blk(LANES), blk(ATT_WIDTH), whole(IDX_DIM), whole(KV_WIDTH), whole(KV_WIDTH)],
        out_specs=blk(ATT_WIDTH),
        out_shape=jax.ShapeDtypeStruct((b, s, ATT_WIDTH), f32),
        scratch_shapes=[pltpu.VMEM((Q_BLOCK, s), i32)],
        compiler_params=_cparams(("arbitrary", "arbitrary")),
        name="attn_prompt",
    )(qi, wi, q, ki, k, v)


def _attn_sample_kernel(pt_ref, qi_ref, wi_ref, q_ref, kin_ref, kn_ref, vn_ref, cki_ref, ck_ref, cv_ref,
                        o_ref, ki_buf, k_buf, v_buf, key_ref, sem, *, n_pages, dec_seq, topk):
    b = pl.program_id(0)
    past = n_pages * PAGE_SIZE
    n_keys = past + LANES

    def copies(p):
        page = pt_ref[b * n_pages + p]
        rows = pl.ds(pl.multiple_of(p * PAGE_SIZE, PAGE_SIZE), PAGE_SIZE)
        return (pltpu.make_async_copy(cki_ref.at[page], ki_buf.at[rows, :], sem.at[0]),
                pltpu.make_async_copy(ck_ref.at[page], k_buf.at[rows, :], sem.at[1]),
                pltpu.make_async_copy(cv_ref.at[page], v_buf.at[rows, :], sem.at[2]))

    def start(p, c):
        for cp in copies(p):
            cp.start()
        return c

    def wait(p, c):
        for cp in copies(p):
            cp.wait()
        return c

    lax.fori_loop(0, n_pages, start, 0)
    ki_buf[past:n_keys, :] = kin_ref[...]
    k_buf[past:n_keys, :] = kn_ref[...]
    v_buf[past:n_keys, :] = vn_ref[...]
    lax.fori_loop(0, n_pages, wait, 0)

    ki = ki_buf[...]
    wi = wi_ref[...] * IDX_W_SCALE
    qi_rows = jnp.concatenate([qi_ref[:, h * IDX_DIM:(h + 1) * IDX_DIM] for h in range(IDX_HEADS)], axis=0)
    dots = _mm_nt(qi_rows, ki)
    score = jnp.zeros((SUBLANES, n_keys), f32)
    for h in range(IDX_HEADS):
        score = score + wi[:, h:h + 1] * jnp.maximum(dots[h * SUBLANES:(h + 1) * SUBLANES, :], 0.0)
    col = lax.broadcasted_iota(i32, (SUBLANES, n_keys), 1)
    tok = lax.broadcasted_iota(i32, (SUBLANES, n_keys), 0)
    allowed = (col < past) | ((col - past <= tok) & (col - past < dec_seq))
    key_ref[...] = jnp.where(allowed, _sort_key(score), INT_MIN)
    sel = _select_topk(key_ref, n_keys, topk)
    sel2 = jnp.concatenate([sel] * KV_GROUPS, axis=0)
    scale = HEAD_DIM ** -0.5
    for g in range(N_KV_HEADS):
        qg = jnp.concatenate([q_ref[:, h * HEAD_DIM:(h + 1) * HEAD_DIM]
                              for h in range(g * KV_GROUPS, (g + 1) * KV_GROUPS)], axis=0) * scale
        o = _masked_attention(qg, k_buf[:, g * HEAD_DIM:(g + 1) * HEAD_DIM],
                              v_buf[:, g * HEAD_DIM:(g + 1) * HEAD_DIM], sel2)
        for r in range(KV_GROUPS):
            h = g * KV_GROUPS + r
            o_ref[:, h * HEAD_DIM:(h + 1) * HEAD_DIM] = o[r * SUBLANES:(r + 1) * SUBLANES, :]


def _attn_sample(page_table, qi8, wi8, q8, ki_new, k_new, v_new, cache_ki, cache_k, cache_v, dec_seq):
    db, n_pages = page_table.shape
    past = n_pages * PAGE_SIZE
    n_keys = past + LANES
    topk = min(TOPK_MAX, (past + dec_seq) // 4)
    per = lambda r, w: pl.BlockSpec((None, r, w), lambda bi, pt: (bi, 0, 0))
    hbm = pl.BlockSpec(memory_space=pl.ANY)
    return pl.pallas_call(
        functools.partial(_attn_sample_kernel, n_pages=n_pages, dec_seq=dec_seq, topk=topk),
        grid_spec=pltpu.PrefetchScalarGridSpec(
            num_scalar_prefetch=1, grid=(db,),
            in_specs=[per(SUBLANES, IDX_HEADS * IDX_DIM), per(SUBLANES, LANES), per(SUBLANES, ATT_WIDTH),
                      per(LANES, IDX_DIM), per(LANES, KV_WIDTH), per(LANES, KV_WIDTH), hbm, hbm, hbm],
            out_specs=per(SUBLANES, ATT_WIDTH),
            scratch_shapes=[pltpu.VMEM((n_keys, IDX_DIM), f32), pltpu.VMEM((n_keys, KV_WIDTH), f32),
                            pltpu.VMEM((n_keys, KV_WIDTH), f32), pltpu.VMEM((SUBLANES, n_keys), i32),
                            pltpu.SemaphoreType.DMA((3,))]),
        out_shape=jax.ShapeDtypeStruct((db, SUBLANES, ATT_WIDTH), f32),
        compiler_params=_cparams(("arbitrary",)),
        name="attn_sample",
    )(page_table.reshape(-1), qi8, wi8, q8, ki_new, k_new, v_new, cache_ki, cache_k, cache_v)


def _post_body(x_ref, att_ref, conv_ref, sc1_ref, sh1_ref, g1_ref, sc2_ref, sh2_ref,
               wg_ref, wa_ref, wb_ref, wo_ref, wr_ref, br_ref, lng_ref, lnb_ref, cnt_in_ref,
               x1_ref, h2_ref, tope_ref, comb_ref, rank_ref, cnt_ref, *, alpha):
    del cnt_in_ref
    d = x_ref.shape[-1]
    x = x_ref[...]
    hb = (x * (1.0 + sc1_ref[...]) + sh1_ref[...]).astype(bf16)
    ga = jnp.dot(hb, wg_ref[:, 0:d], preferred_element_type=f32)
    gb = jnp.dot(hb, wg_ref[:, d:2 * d], preferred_element_type=f32)
    a = _mm(att_ref[...], wa_ref[...])
    c = _mm(conv_ref[...], wb_ref[...])
    mix = _mm(jax.nn.sigmoid(ga) * a + jax.nn.sigmoid(gb) * c, wo_ref[...])
    x1 = _layer_norm(alpha * x + g1_ref[...] * mix, lng_ref[...], lnb_ref[...])
    x1_ref[...] = x1
    h2 = x1 * (1.0 + sc2_ref[...]) + sh2_ref[...]
    h2_ref[...] = h2
    logits = _mm(h2, wr_ref[...]) + br_ref[...]

    tm = logits.shape[0]
    lane = lax.broadcasted_iota(i32, (tm, LANES), 1)
    tope = jnp.zeros((tm, LANES), i32)
    topv = jnp.full((tm, LANES), -jnp.inf, f32)
    hot = jnp.zeros((tm, LANES), f32)
    picks = []
    work = logits
    for k in range(TOP_K):
        m = jnp.max(work, axis=1, keepdims=True)
        idx = jnp.min(jnp.where(work == m, lane, LANES), axis=1, keepdims=True)
        hit = lane == idx
        tope = jnp.where(lane == k, idx, tope)
        topv = jnp.where(lane == k, m, topv)
        hot = hot + jnp.where(hit, 1.0, 0.0)
        work = jnp.where(hit, -jnp.inf, work)
        picks.append(idx)
    e = jnp.where(lane < TOP_K, jnp.exp(topv - jnp.max(topv, axis=1, keepdims=True)), 0.0)
    comb_ref[...] = e / jnp.sum(e, axis=1, keepdims=True)
    tope_ref[...] = tope

    r_i = lax.broadcasted_iota(i32, (tm, tm), 0)
    c_i = lax.broadcasted_iota(i32, (tm, tm), 1)
    lower = jnp.where(c_i < r_i, 1.0, 0.0)
    before = _mm(lower, hot) + cnt_ref[...]
    rank = jnp.zeros((tm, LANES), f32)
    for k in range(TOP_K):
        rk = jnp.sum(jnp.where(lane == picks[k], before, 0.0), axis=1, keepdims=True)
        rank = jnp.where(lane == k, rk, rank)
    rank_ref[...] = rank.astype(i32)
    cnt_ref[...] = cnt_ref[...] + jnp.sum(hot, axis=0, keepdims=True)


def _post(x, att, conv, mods, weights, cnt_in, alpha, per_row_mods):
    wg, wa, wb, wo, wr, br, lng, lnb = weights
    d = x.shape[-1]
    if per_row_mods:
        n = x.shape[0]
        grid = (1,)
        row = lambda w: pl.BlockSpec((n, w), lambda i: (0, 0))
        mod = row(d)
        const = lambda shp: pl.BlockSpec(shp, lambda i: (0,) * len(shp))
        lead = (n,)
        first = lambda: pl.program_id(0) == 0
        sem = ("arbitrary",)
    else:
        b, s, _ = x.shape
        tm = min(ROW_TILE, s)
        grid = (b, s // tm)
        row = lambda w: pl.BlockSpec((None, tm, w), lambda bi, i: (bi, i, 0))
        mod = pl.BlockSpec((None, 1, d), lambda bi, i: (bi, 0, 0))
        const = lambda shp: pl.BlockSpec(shp, lambda bi, i: (0,) * len(shp))
        lead = (b, s)
        first = lambda: (pl.program_id(0) == 0) & (pl.program_id(1) == 0)
        sem = ("arbitrary", "arbitrary")

    def body(*refs):
        cnt_in_ref, cnt_ref = refs[16], refs[22]

        @pl.when(first())
        def _():
            cnt_ref[...] = cnt_in_ref[...]

        _post_body(*refs, alpha=alpha)

    outs = [jax.ShapeDtypeStruct(lead + (d,), f32), jax.ShapeDtypeStruct(lead + (d,), f32),
            jax.ShapeDtypeStruct(lead + (LANES,), i32), jax.ShapeDtypeStruct(lead + (LANES,), f32),
            jax.ShapeDtypeStruct(lead + (LANES,), i32), jax.ShapeDtypeStruct((1, LANES), f32)]
    return pl.pallas_call(
        body,
        grid=grid,
        in_specs=[row(d), row(ATT_WIDTH), row(512), mod, mod, mod, mod, mod,
                  const(wg.shape), const(wa.shape), const(wb.shape), const(wo.shape), const(wr.shape),
                  const(br.shape), const(lng.shape), const(lnb.shape), const((1, LANES))],
        out_specs=[row(d), row(d), row(LANES), row(LANES), row(LANES), const((1, LANES))],
        out_shape=outs,
        compiler_params=_cparams(sem),
        name="post_rows" if per_row_mods else "post_prompt",
    )(x, att, conv, *mods, wg, wa, wb, wo, wr, br, lng, lnb, cnt_in)


def _dispatch_kernel(pos_ref, pend_ref, hp_ref, hs_ref, xs_ref, zero_ref, sem, zsem, *, n_prompt_steps, n_blocks):
    tm = hp_ref.shape[0]
    step = pl.program_id(0)

    @pl.when(step == 0)
    def _():
        zero_ref[...] = jnp.zeros_like(zero_ref)

        def zero_block(row0, wait):
            dst = xs_ref.at[pl.ds(pl.multiple_of(row0, MOE_ROWS), MOE_ROWS), :]
            cp = pltpu.make_async_copy(zero_ref, dst, zsem)
            if wait:
                cp.wait()
            else:
                cp.start()

        def fill(wait):
            def last_block(e, c):
                end = pend_ref[e]
                begin = jnp.where(e == 0, 0, pend_ref[jnp.maximum(e - 1, 0)])

                @pl.when(end > begin)
                def _():
                    zero_block(end - MOE_ROWS, wait)

                return c

            lax.fori_loop(0, N_EXPERTS, last_block, 0)
            n_used = pend_ref[N_EXPERTS - 1] // MOE_ROWS
            lax.fori_loop(n_used, n_blocks, lambda blk, c: (zero_block(blk * MOE_ROWS, wait), c)[1], 0)

        fill(False)
        fill(True)

    def scatter(src_ref, base):
        def issue(r, c):
            for k in range(TOP_K):
                p = pos_ref[(base + r) * TOP_K + k]
                pltpu.make_async_copy(src_ref.at[pl.ds(r, 1), :], xs_ref.at[pl.ds(p, 1), :], sem).start()
            return c

        def drain(r, c):
            for k in range(TOP_K):
                pltpu.make_async_copy(src_ref.at[pl.ds(0, 1), :], xs_ref.at[pl.ds(0, 1), :], sem).wait()
            return c

        lax.fori_loop(0, src_ref.shape[0], issue, 0)
        lax.fori_loop(0, src_ref.shape[0], drain, 0)

    @pl.when(step < n_prompt_steps)
    def _():
        scatter(hp_ref, step * tm)

    @pl.when(step == n_prompt_steps)
    def _():
        scatter(hs_ref, n_prompt_steps * tm)


def _dispatch(pos, pend, h_prompt, h_sample, n_blocks):
    n, d = h_prompt.shape
    ns = h_sample.shape[0]
    tm = min(ROW_TILE, n)
    steps = n // tm
    return pl.pallas_call(
        functools.partial(_dispatch_kernel, n_prompt_steps=steps, n_blocks=n_blocks),
        grid_spec=pltpu.PrefetchScalarGridSpec(
            num_scalar_prefetch=2, grid=(steps + 1,),
            in_specs=[pl.BlockSpec((tm, d), lambda i, pos, pend: (jnp.minimum(i, steps - 1), 0)),
                      pl.BlockSpec((ns, d), lambda i, pos, pend: (0, 0))],
            out_specs=pl.BlockSpec(memory_space=pl.ANY),
            scratch_shapes=[pltpu.VMEM((MOE_ROWS, d), f32), pltpu.SemaphoreType.DMA(()), pltpu.SemaphoreType.DMA(())]),
        out_shape=jax.ShapeDtypeStruct((n_blocks * MOE_ROWS, d), f32),
        compiler_params=_cparams(("arbitrary",)),
        name="moe_dispatch",
    )(pos, pend, h_prompt, h_sample)


def _expert_kernel(be_ref, nu_ref, xs_ref, wgu_ref, bgu_ref, wd_ref, bd_ref, y_ref, wgu_bf, wd_bf):
    b = pl.program_id(0)

    @pl.when(b < nu_ref[0])
    def _():
        @pl.when((b == 0) | (be_ref[b] != be_ref[jnp.maximum(b - 1, 0)]))
        def _():
            wgu_bf[...] = wgu_ref[...].astype(bf16)
            wd_bf[...] = wd_ref[...].astype(bf16)

        ff = wd_bf.shape[0]
        gu = jnp.dot(xs_ref[...].astype(bf16), wgu_bf[...], preferred_element_type=f32) + bgu_ref[...]
        gate = jnp.minimum(gu[:, 0:ff], SWIGLU_LIMIT)
        up = jnp.clip(gu[:, ff:2 * ff], -SWIGLU_LIMIT, SWIGLU_LIMIT)
        act = gate * jax.nn.sigmoid(SWIGLU_ALPHA * gate) * (up + 1.0)
        y_ref[...] = jnp.dot(act.astype(bf16), wd_bf[...], preferred_element_type=f32) + bd_ref[...]

    @pl.when(b >= nu_ref[0])
    def _():
        y_ref[...] = jnp.zeros_like(y_ref)


def _experts(block_e, n_used, xs, w_gate_up, b_gate_up, w_down, b_down):
    n_rows, d = xs.shape
    n_exp, _, ff2 = w_gate_up.shape
    ff = ff2 // 2
    rows = lambda b, be, nu: (b, 0)
    per_e = lambda b, be, nu: (be[b], 0, 0)
    return pl.pallas_call(
        _expert_kernel,
        grid_spec=pltpu.PrefetchScalarGridSpec(
            num_scalar_prefetch=2, grid=(n_rows // MOE_ROWS,),
            in_specs=[pl.BlockSpec((MOE_ROWS, d), rows),
                      pl.BlockSpec((None, d, ff2), per_e), pl.BlockSpec((None, 1, ff2), per_e),
                      pl.BlockSpec((None, ff, d), per_e), pl.BlockSpec((None, 1, d), per_e)],
            out_specs=pl.BlockSpec((MOE_ROWS, d), rows),
            scratch_shapes=[pltpu.VMEM((d, ff2), bf16), pltpu.VMEM((ff, d), bf16)]),
        out_shape=jax.ShapeDtypeStruct((n_rows, d), f32),
        compiler_params=_cparams(("arbitrary",)),
        name="moe_experts",
    )(block_e, n_used, xs, w_gate_up, b_gate_up.reshape(n_exp, 1, ff2), w_down, b_down.reshape(n_exp, 1, d))


def _combine_kernel(pos_ref, y_ref, x1_ref, comb_ref, g2_ref, lng_ref, lnb_ref, o_ref, buf, sem, *, alpha, n_steps, step_fn):
    tc = x1_ref.shape[0]
    step = step_fn()

    def rows(s, slot, wait):
        base = s * tc

        def body(r, c):
            for k in range(TOP_K):
                p = 0 if wait else pos_ref[(base + r) * TOP_K + k]
                cp = pltpu.make_async_copy(y_ref.at[pl.ds(p, 1), :], buf.at[slot, k, pl.ds(0 if wait else r, 1), :],
                                           sem.at[slot])
                if wait:
                    cp.wait()
                else:
                    cp.start()
            return c

        lax.fori_loop(0, tc, body, 0)

    @pl.when(step == 0)
    def _():
        rows(0, 0, False)

    @pl.when(step + 1 < n_steps)
    def _():
        rows(step + 1, (step + 1) % 2, False)

    slot = step % 2
    rows(step, slot, True)
    comb = comb_ref[...]
    ffn = comb[:, 0:1] * buf[slot, 0]
    for k in range(1, TOP_K):
        ffn = ffn + comb[:, k:k + 1] * buf[slot, k]
    o_ref[...] = _layer_norm(alpha * x1_ref[...] + g2_ref[...] * ffn, lng_ref[...], lnb_ref[...])


def _combine(pos, y, x1, comb, g2, lng, lnb, alpha, per_row_mods):
    d = x1.shape[-1]
    hbm = pl.BlockSpec(memory_space=pl.ANY)
    if per_row_mods:
        n = x1.shape[0]
        tc = min(COMBINE_TILE, n)
        grid = (n // tc,)
        row = lambda w: pl.BlockSpec((tc, w), lambda i, pos: (i, 0))
        mod = row(d)
        const = lambda shp: pl.BlockSpec(shp, lambda i, pos: (0,) * len(shp))
        step_fn = lambda: pl.program_id(0)
        n_steps = n // tc
        sem = ("arbitrary",)
        out_shape = jax.ShapeDtypeStruct((n, d), f32)
    else:
        b, s, _ = x1.shape
        tc = min(COMBINE_TILE, s)
        grid = (b, s // tc)
        row = lambda w: pl.BlockSpec((None, tc, w), lambda bi, i, pos: (bi, i, 0))
        mod = pl.BlockSpec((None, 1, d), lambda bi, i, pos: (bi, 0, 0))
        const = lambda shp: pl.BlockSpec(shp, lambda bi, i, pos: (0,) * len(shp))
        per_b = s // tc
        step_fn = lambda: pl.program_id(0) * per_b + pl.program_id(1)
        n_steps = b * per_b
        sem = ("arbitrary", "arbitrary")
        out_shape = jax.ShapeDtypeStruct((b, s, d), f32)
    return pl.pallas_call(
        functools.partial(_combine_kernel, alpha=alpha, n_steps=n_steps, step_fn=step_fn),
        grid_spec=pltpu.PrefetchScalarGridSpec(
            num_scalar_prefetch=1, grid=grid,
            in_specs=[hbm, row(d), row(LANES), mod, const((1, d)), const((1, d))],
            out_specs=row(d),
            scratch_shapes=[pltpu.VMEM((2, TOP_K, tc, d), f32), pltpu.SemaphoreType.DMA((2,))]),
        out_shape=out_shape,
        compiler_params=_cparams(sem),
        name="moe_combine_rows" if per_row_mods else "moe_combine_prompt",
    )(pos, y, x1, comb, g2, lng, lnb)


def _rope_tables(pos):
    n = pos.shape[0]
    inv_freq = ROPE_THETA ** (-jnp.arange(ROPE_HALF, dtype=f32) * 2.0 / ROPE_DIM)
    ang = pos[:, None] * inv_freq[None, :]
    cos, sin = jnp.cos(ang), jnp.sin(ang)
    rest = HEAD_DIM - ROPE_DIM
    z8, zr = jnp.zeros((n, ROPE_HALF), f32), jnp.zeros((n, rest), f32)
    c = jnp.concatenate([cos, cos, jnp.ones((n, rest), f32)], axis=1)
    s1 = jnp.concatenate([-sin, z8, zr], axis=1)
    s2 = jnp.concatenate([z8, sin, zr], axis=1)
    reps = LANES // HEAD_DIM
    return tuple(jnp.tile(a, (1, reps)) for a in (c, s1, s2))


def _pad_cols(a, width, value=0.0):
    return jnp.pad(a, ((0, 0), (0, width - a.shape[1])), constant_values=value)


def _layer(xp, xs, c_prompt, c_sample, cache_k, cache_v, cache_ki, state_conv, page_table, p, alpha):
    b, s, d = xp.shape
    db, t, _ = xs.shape
    ns = db * t
    assert t >= CONV_K - 1 and ns % SUBLANES == 0 and t <= SUBLANES

    n_c = b + db
    c_all = jnp.pad(jnp.concatenate([c_prompt, c_sample], axis=0), ((0, -n_c % SUBLANES), (0, 0)))
    ada = _ada(c_all, p["w_ada"], p["b_ada"])
    mods = [ada[:, j * d:(j + 1) * d] for j in range(6)]
    mp = [m[:b].reshape(b, 1, d) for m in mods]
    ms = [jnp.repeat(m[b:b + db], t, axis=0) for m in mods]

    sizes = (ATT_WIDTH, KV_WIDTH, KV_WIDTH, IDX_HEADS * IDX_DIM, IDX_DIM, IDX_HEADS, d // 2, d // 2, d // 2, d, d)
    cuts = [int(v) for v in np.cumsum(sizes)[:-1]]
    wq, wk, wv, wqi, wki, wwi, wxin, wbg, wcg, wga, wgb = jnp.split(p["w_in"], cuts, axis=1)
    w_proj = jnp.concatenate([wq, wk, wv, wqi, _pad_cols(wki, LANES), _pad_cols(wwi, LANES), wxin, wbg, wcg],
                             axis=1).astype(bf16)
    wg = jnp.concatenate([wga, wgb], axis=1).astype(bf16)
    post_w = (wg, p["w_branch_a"].astype(bf16), p["w_branch_b"].astype(bf16), p["w_o"].astype(bf16),
              _pad_cols(p["w_router"], LANES).astype(bf16), _pad_cols(p["b_router"][None, :], LANES, NEG_BIG),
              p["ln1_g"][None, :], p["ln1_b"][None, :])
    ln2 = (p["ln2_g"][None, :], p["ln2_b"][None, :])

    tabs_p = _rope_tables(jnp.arange(s, dtype=f32))
    q, k, v, qi, ki, wi, conv_p, cst_p = _inproj_prompt(xp, mp[1], mp[0], tabs_p, w_proj, p["conv_w"])
    att_p = _attn_prompt(qi, wi, q, ki, k, v)

    past = page_table.shape[1] * PAGE_SIZE
    tabs_s = _rope_tables(jnp.tile(past + jnp.arange(t, dtype=f32), db))
    zeros = jnp.zeros((db, t, d // 2), f32)
    p1 = zeros.at[:, 0].set(state_conv[:, 1]).reshape(ns, d // 2)
    p2 = zeros.at[:, 0].set(state_conv[:, 0]).at[:, 1].set(state_conv[:, 1]).reshape(ns, d // 2)
    qs, ks, vs, qis, kis, wis, conv_s, u_s = _inproj_sample(xs.reshape(ns, d), ms[1], ms[0], tabs_s, w_proj,
                                                            p["conv_w"], p1, p2, t)
    pad_t = lambda a, rows: jnp.pad(a.reshape(db, t, -1), ((0, 0), (0, rows - t), (0, 0)))
    n_pool = cache_k.shape[0]
    att_s = _attn_sample(page_table, pad_t(qis, SUBLANES), pad_t(wis, SUBLANES), pad_t(qs, SUBLANES),
                         pad_t(kis, LANES), pad_t(ks, LANES), pad_t(vs, LANES),
                         cache_ki, cache_k.reshape(n_pool, PAGE_SIZE, KV_WIDTH),
                         cache_v.reshape(n_pool, PAGE_SIZE, KV_WIDTH), t)[:, :t].reshape(ns, ATT_WIDTH)

    cnt0 = jnp.zeros((1, LANES), f32)
    x1p, h2p, tope_p, comb_p, rank_p, cnt1 = _post(xp, att_p, conv_p, (mp[1], mp[0], mp[2], mp[4], mp[3]),
                                                   post_w, cnt0, alpha, False)
    x1s, h2s, tope_s, comb_s, rank_s, cnt2 = _post(xs.reshape(ns, d), att_s, conv_s, (ms[1], ms[0], ms[2], ms[4], ms[3]),
                                                   post_w, cnt1, alpha, True)

    n_p = b * s
    counts = cnt2[0, :N_EXPERTS].astype(i32)
    padded = (counts + MOE_ROWS - 1) // MOE_ROWS * MOE_ROWS
    pend = jnp.cumsum(padded).astype(i32)
    pstart = pend - padded
    tope = jnp.concatenate([tope_p.reshape(n_p, LANES)[:, :TOP_K], tope_s[:, :TOP_K]], axis=0)
    rank = jnp.concatenate([rank_p.reshape(n_p, LANES)[:, :TOP_K], rank_s[:, :TOP_K]], axis=0)
    pos = (pstart[tope] + rank).astype(i32)
    n_blocks = ((n_p + ns) * TOP_K + N_EXPERTS * (MOE_ROWS - 1)) // MOE_ROWS
    n_used = pend[-1:] // MOE_ROWS
    blk = jnp.minimum(jnp.arange(n_blocks, dtype=i32), n_used[0] - 1)
    block_e = jnp.minimum(jnp.searchsorted(pend, blk * MOE_ROWS, side="right"), N_EXPERTS - 1).astype(i32)
    pos_p, pos_s = pos[:n_p].reshape(-1), pos[n_p:].reshape(-1)

    rows = _dispatch(pos.reshape(-1), pend, h2p.reshape(n_p, d), h2s, n_blocks)
    y = _experts(block_e, n_used, rows, p["w_gate_up"], p["b_gate_up"], p["w_down"], p["b_down"])
    yp = _combine(pos_p, y, x1p, comb_p, mp[5], *ln2, alpha, False)
    ys = _combine(pos_s, y, x1s, comb_s, ms[5], *ln2, alpha, True).reshape(db, t, d)

    new_p = (k.reshape(b, s, N_KV_HEADS, HEAD_DIM), v.reshape(b, s, N_KV_HEADS, HEAD_DIM), ki, cst_p)
    new_s = (ks.reshape(db, t, N_KV_HEADS, HEAD_DIM), vs.reshape(db, t, N_KV_HEADS, HEAD_DIM),
             kis.reshape(db, t, IDX_DIM), u_s.reshape(db, t, d // 2)[:, t - (CONV_K - 1):])
    return yp, ys, new_p, new_s


def kernel(x_prompt, x_sample, cache_k, cache_v, cache_idx_k, state_conv, page_table, c_prompt, c_sample,
           w_ada, b_ada, w_in, conv_w, w_branch_a, w_branch_b, w_o, ln1_g, ln1_b, ln2_g, ln2_b,
           w_router, b_router, w_gate_up, b_gate_up, w_down, b_down):
    depth = w_ada.shape[0]
    alpha = float((2 * depth) ** 0.25)
    xp, xs = x_prompt, x_sample
    acc_p, acc_s = [], []
    for l in range(depth):
        p = {"w_ada": w_ada[l], "b_ada": b_ada[l], "w_in": w_in[l], "conv_w": conv_w[l],
             "w_branch_a": w_branch_a[l], "w_branch_b": w_branch_b[l], "w_o": w_o[l],
             "ln1_g": ln1_g[l], "ln1_b": ln1_b[l], "ln2_g": ln2_g[l], "ln2_b": ln2_b[l],
             "w_router": w_router[l], "b_router": b_router[l], "w_gate_up": w_gate_up[l],
             "b_gate_up": b_gate_up[l], "w_down": w_down[l], "b_down": b_down[l]}
        xp, xs, new_p, new_s = _layer(xp, xs, c_prompt, c_sample, cache_k[l], cache_v[l], cache_idx_k[l],
                                      state_conv[l], page_table, p, alpha)
        acc_p.append(new_p)
        acc_s.append(new_s)
    stack = lambda acc, j: jnp.stack([a[j] for a in acc])
    return (xp, xs, stack(acc_p, 0), stack(acc_p, 1), stack(acc_p, 2), stack(acc_p, 3),
            stack(acc_s, 0), stack(acc_s, 1), stack(acc_s, 2), stack(acc_s, 3))
```

```python
import functools

import jax
import jax.numpy as jnp
import numpy as np
from jax import lax
from jax.experimental import pallas as pl
from jax.experimental.pallas import tpu as pltpu

f32 = jnp.float32
bf16 = jnp.bfloat16
i32 = jnp.int32

N_HEADS = 8
HEAD_DIM = 64
N_KV_HEADS = 4
KV_GROUPS = N_HEADS // N_KV_HEADS
ATT_WIDTH = N_HEADS * HEAD_DIM
KV_WIDTH = N_KV_HEADS * HEAD_DIM
ROPE_DIM = HEAD_DIM // 4
ROPE_HALF = ROPE_DIM // 2
ROPE_THETA = 500000.0
IDX_HEADS = 8
IDX_DIM = 64
IDX_W_SCALE = (IDX_HEADS * IDX_DIM) ** -0.5
TOPK_MAX = 256
Q_BLOCK = 128
PAGE_SIZE = 128
CONV_K = 3
N_EXPERTS = 32
TOP_K = 4
SWIGLU_LIMIT = 7.0
SWIGLU_ALPHA = 1.702
LN_EPS = 1e-5

LANES = 128
SUBLANES = 8
VMEM_LIMIT = 56 * 1024 * 1024

ROW_TILE = 256
MOE_ROWS = 256
COMBINE_TILE = 128
KEY_CLASS = 1024
SCORE_CHUNK = 512
SEQ_GROUP = 8
INT_MIN = int(np.iinfo(np.int32).min)
NEG_BIG = -1e30


def _cparams(sem):
    return pltpu.CompilerParams(dimension_semantics=sem, vmem_limit_bytes=VMEM_LIMIT)


def _mm(a, b):
    return jnp.dot(a.astype(bf16), b.astype(bf16), preferred_element_type=f32)


def _mm_nt(a, b):
    return lax.dot_general(a.astype(bf16), b.astype(bf16), (((1,), (1,)), ((), ())), preferred_element_type=f32)


def _layer_norm(y, g, b):
    mu = jnp.mean(y, axis=-1, keepdims=True)
    d = y - mu
    var = jnp.mean(d * d, axis=-1, keepdims=True)
    return d * lax.rsqrt(var + LN_EPS) * g + b


def _ada_kernel(c_ref, w_ref, b_ref, o_ref):
    o_ref[...] = _mm(c_ref[...], w_ref[...]) + b_ref[...]


def _ada(c_all, w_ada, b_ada):
    n, d = c_all.shape
    width = w_ada.shape[1]
    return pl.pallas_call(
        _ada_kernel,
        grid=(width // d,),
        in_specs=[pl.BlockSpec((n, d), lambda j: (0, 0)),
                  pl.BlockSpec((d, d), lambda j: (0, j)),
                  pl.BlockSpec((1, d), lambda j: (0, j))],
        out_specs=pl.BlockSpec((n, d), lambda j: (0, j)),
        out_shape=jax.ShapeDtypeStruct((n, width), f32),
        compiler_params=_cparams(("arbitrary",)),
        name="ada",
    )(c_all, w_ada, b_ada.reshape(1, width))


_SEG = {}
_off = 0
for _name, _w in (("q", ATT_WIDTH), ("k", KV_WIDTH), ("v", KV_WIDTH), ("qi", IDX_HEADS * IDX_DIM),
                  ("ki", LANES), ("wi", LANES), ("xin", 512), ("bg", 512), ("cg", 512)):
    _SEG[_name] = (_off, _off + _w)
    _off += _w
PROJ_WIDTH = _off


def _rope(t, c, s1, s2):
    w = t.shape[1]
    reps = w // LANES
    if reps > 1:
        c, s1, s2 = (jnp.tile(a, (1, reps)) for a in (c, s1, s2))
    return t * c + pltpu.roll(t, w - ROPE_HALF, 1) * s1 + pltpu.roll(t, ROPE_HALF, 1) * s2


def _project(x_ref, sc_ref, sh_ref, cos_ref, s1_ref, s2_ref, w_ref):
    h = x_ref[...] * (1.0 + sc_ref[...]) + sh_ref[...]
    hb = h.astype(bf16)
    c, s1, s2 = cos_ref[...], s1_ref[...], s2_ref[...]

    def seg(name):
        a, b = _SEG[name]
        return jnp.dot(hb, w_ref[:, a:b], preferred_element_type=f32)

    q = _rope(seg("q"), c, s1, s2)
    k = _rope(seg("k"), c, s1, s2)
    v = seg("v")
    qi = _rope(seg("qi"), c, s1, s2)
    ki = _rope(seg("ki"), c, s1, s2)[:, :IDX_DIM]
    wi = seg("wi")
    u = seg("cg") * seg("xin")
    return q, k, v, qi, ki, wi, u, seg("bg")


def _conv_out(u, bg, cw_ref, sh1, sh2):
    y = cw_ref[0:1, :] * sh2 + cw_ref[1:2, :] * sh1
    y = y + cw_ref[2:3, :] * u
    return bg * y


def _rope_rows(t, cos_t, sin_t, n_heads):
    parts = []
    for g in range(n_heads):
        base = g * HEAD_DIM
        r0, r1 = t[base:base + ROPE_HALF], t[base + ROPE_HALF:base + ROPE_DIM]
        parts += [r0 * cos_t - r1 * sin_t, r1 * cos_t + r0 * sin_t, t[base + ROPE_DIM:base + HEAD_DIM]]
    return jnp.concatenate(parts, axis=0)


def _inproj_prompt_kernel(x_ref, sc_ref, sh_ref, cos_ref, s1_ref, s2_ref, cost_ref, sint_ref, w_ref, wt_ref, cw_ref,
                          q_ref, qi_ref, wi_ref, kt_ref, vt_ref, kit_ref, ktb_ref, vtb_ref, kitb_ref,
                          conv_ref, cst_ref, carry_ref):
    @pl.when(pl.program_id(1) == 0)
    def _():
        carry_ref[...] = jnp.zeros_like(carry_ref)

    hb = (x_ref[...] * (1.0 + sc_ref[...]) + sh_ref[...]).astype(bf16)
    c, s1, s2 = cos_ref[...], s1_ref[...], s2_ref[...]

    def seg(name):
        a, b = _SEG[name]
        return jnp.dot(hb, w_ref[:, a:b], preferred_element_type=f32)

    q = _rope(seg("q"), c, s1, s2) * HEAD_DIM ** -0.5
    qi = _rope(seg("qi"), c, s1, s2)
    for h in range(N_HEADS):
        q_ref[h] = q[:, h * HEAD_DIM:(h + 1) * HEAD_DIM].astype(bf16)
    for h in range(IDX_HEADS):
        qi_ref[h] = qi[:, h * IDX_DIM:(h + 1) * IDX_DIM].astype(bf16)
    wi_ref[...] = seg("wi") * IDX_W_SCALE

    t = lax.dot_general(wt_ref[...], hb, (((1,), (1,)), ((), ())), preferred_element_type=f32)
    cos_t, sin_t = cost_ref[...], sint_ref[...]
    kt = _rope_rows(t[0:KV_WIDTH], cos_t, sin_t, N_KV_HEADS)
    vt = t[KV_WIDTH:2 * KV_WIDTH]
    kit = _rope_rows(t[2 * KV_WIDTH:2 * KV_WIDTH + IDX_DIM], cos_t, sin_t, 1)
    kt_ref[...], vt_ref[...], kit_ref[...] = kt, vt, kit
    ktb_ref[...], vtb_ref[...], kitb_ref[...] = kt.astype(bf16), vt.astype(bf16), kit.astype(bf16)

    u = seg("cg") * seg("xin")
    bg = seg("bg")
    tm = u.shape[0]
    rows = lax.broadcasted_iota(i32, u.shape, 0)
    c0, c1 = carry_ref[0:1, :], carry_ref[1:2, :]
    sh1 = jnp.where(rows == 0, c1, pltpu.roll(u, 1, 0))
    sh2 = jnp.where(rows == 0, c0, jnp.where(rows == 1, c1, pltpu.roll(u, 2, 0)))
    conv_ref[...] = _conv_out(u, bg, cw_ref, sh1, sh2)
    tail = u[tm - 2:tm, :]
    carry_ref[0:2, :] = tail
    cst_ref[...] = tail


def _inproj_sample_kernel(x_ref, sc_ref, sh_ref, cos_ref, s1_ref, s2_ref, w_ref, cw_ref, p1_ref, p2_ref,
                          q_ref, k_ref, v_ref, qi_ref, ki_ref, wi_ref, conv_ref, u_ref, *, dec_seq):
    q, k, v, qi, ki, wi, u, bg = _project(x_ref, sc_ref, sh_ref, cos_ref, s1_ref, s2_ref, w_ref)
    q_ref[...], k_ref[...], v_ref[...], qi_ref[...], ki_ref[...], wi_ref[...] = q, k, v, qi, ki, wi
    t = lax.broadcasted_iota(i32, u.shape, 0) % dec_seq
    sh1 = jnp.where(t == 0, p1_ref[...], pltpu.roll(u, 1, 0))
    sh2 = jnp.where(t < 2, p2_ref[...], pltpu.roll(u, 2, 0))
    conv_ref[...] = _conv_out(u, bg, cw_ref, sh1, sh2)
    u_ref[...] = u


def _inproj_prompt(x, sc1, sh1, tabs, tabs_t, w_proj, w_t, conv_w):
    b, s, d = x.shape
    tm = min(ROW_TILE, s)
    row = lambda w: pl.BlockSpec((None, tm, w), lambda bi, i: (bi, i, 0))
    heads = lambda n: pl.BlockSpec((None, n, tm, HEAD_DIM), lambda bi, i: (bi, 0, i, 0))
    col = lambda r: pl.BlockSpec((None, r, tm), lambda bi, i: (bi, 0, i))
    per_b = pl.BlockSpec((None, 1, d), lambda bi, i: (bi, 0, 0))
    tab = pl.BlockSpec((tm, LANES), lambda bi, i: (i, 0))
    tab_t = pl.BlockSpec((ROPE_HALF, tm), lambda bi, i: (0, i))
    const = lambda a: pl.BlockSpec(a.shape, lambda bi, i: (0,) * a.ndim)
    feat = (KV_WIDTH, KV_WIDTH, IDX_DIM)
    return pl.pallas_call(
        _inproj_prompt_kernel,
        grid=(b, s // tm),
        in_specs=[row(d), per_b, per_b, tab, tab, tab, tab_t, tab_t, const(w_proj), const(w_t), const(conv_w)],
        out_specs=[heads(N_HEADS), heads(IDX_HEADS), row(LANES)] + [col(r) for r in feat] + [col(r) for r in feat]
        + [row(512), pl.BlockSpec((None, CONV_K - 1, 512), lambda bi, i: (bi, 0, 0))],
        out_shape=[jax.ShapeDtypeStruct((b, N_HEADS, s, HEAD_DIM), bf16),
                   jax.ShapeDtypeStruct((b, IDX_HEADS, s, IDX_DIM), bf16),
                   jax.ShapeDtypeStruct((b, s, LANES), f32)]
        + [jax.ShapeDtypeStruct((b, r, s), f32) for r in feat]
        + [jax.ShapeDtypeStruct((b, r, s), bf16) for r in feat]
        + [jax.ShapeDtypeStruct((b, s, 512), f32), jax.ShapeDtypeStruct((b, CONV_K - 1, 512), f32)],
        scratch_shapes=[pltpu.VMEM((SUBLANES, 512), f32)],
        compiler_params=_cparams(("arbitrary", "arbitrary")),
        name="inproj_prompt",
    )(x, sc1, sh1, *tabs, *tabs_t, w_proj, w_t, conv_w)


def _inproj_sample(x, sc1, sh1, tabs, w_proj, conv_w, p1, p2, dec_seq):
    n, d = x.shape
    widths = (ATT_WIDTH, KV_WIDTH, KV_WIDTH, IDX_HEADS * IDX_DIM, IDX_DIM, LANES, 512, 512)
    full = lambda r, w: pl.BlockSpec((r, w), lambda i: (0, 0))
    return pl.pallas_call(
        functools.partial(_inproj_sample_kernel, dec_seq=dec_seq),
        grid=(1,),
        in_specs=[full(n, d), full(n, d), full(n, d), full(n, LANES), full(n, LANES), full(n, LANES),
                  full(d, PROJ_WIDTH), full(CONV_K, 512), full(n, 512), full(n, 512)],
        out_specs=[full(n, w) for w in widths],
        out_shape=[jax.ShapeDtypeStruct((n, w), f32) for w in widths],
        compiler_params=_cparams(("arbitrary",)),
        name="inproj_sample",
    )(x, sc1, sh1, *tabs, w_proj, conv_w, p1, p2)


def _sort_key(score):
    bits = pltpu.bitcast(score + 0.0, i32)
    return bits ^ ((bits >> 31) & jnp.int32(0x7FFFFFFF))


def _select_topk(key_ref, n_cols, topk):
    n_rows = key_ref.shape[0]
    kf = float(topk)

    def count(pred):
        return jnp.sum(jnp.where(pred, 1.0, 0.0), axis=1, keepdims=True)

    t0 = jnp.full((n_rows, 1), INT_MIN, i32)
    zero = jnp.zeros((n_rows, 1), i32)
    t = jnp.where(count(key_ref[:, 0:n_cols] >= zero) >= kf, zero, t0)

    def step(i, t):
        cand = t | jnp.left_shift(jnp.int32(1), 30 - i)
        return jnp.where(count(key_ref[:, 0:n_cols] >= cand) >= kf, cand, t)

    t = lax.fori_loop(0, 31, step, t)
    keys = key_ref[:, 0:n_cols]
    n_ge = count(keys >= t)
    col = lax.broadcasted_iota(i32, (n_rows, n_cols), 1)
    n_bits = int(n_cols).bit_length()

    def tie_search(_):
        need = kf - count(key_ref[:, 0:n_cols] > t)

        def tstep(i, j):
            cand = j + jnp.left_shift(jnp.int32(1), n_bits - 1 - i)
            below = count((key_ref[:, 0:n_cols] == t) & (col < cand))
            return jnp.where(below < need, cand, j)

        return lax.fori_loop(0, n_bits, tstep, jnp.zeros((n_rows, 1), i32))

    jmax = lax.cond(jnp.max(n_ge) > kf, tie_search, lambda _: jnp.full((n_rows, 1), n_cols, i32), 0)
    return (keys > INT_MIN) & ((keys > t) | ((keys == t) & (col <= jmax)))


def _masked_attention(q, kt, vt, bias):
    logits = jnp.dot(q, kt, preferred_element_type=f32) + bias
    m = jnp.max(logits, axis=1, keepdims=True)
    p = jnp.exp(logits - m)
    den = jnp.sum(p, axis=1, keepdims=True)
    o = lax.dot_general(p.astype(bf16), vt, (((1,), (1,)), ((), ())), preferred_element_type=f32)
    return o / den


def _index_scores(qi_rows, wi, kit, n_heads_rows):
    dots = jnp.dot(qi_rows, kit, preferred_element_type=f32)
    r = n_heads_rows
    score = wi[:, 0:1] * jnp.maximum(dots[0:r], 0.0)
    for h in range(1, IDX_HEADS):
        score = score + wi[:, h:h + 1] * jnp.maximum(dots[h * r:(h + 1) * r], 0.0)
    return score


def _attn_prompt_body(qi_ref, wi_ref, q_ref, kit_ref, kt_ref, vt_ref, o_ref, key_ref, s0, n_keys, topk):
    qi_rows = qi_ref[...].reshape(IDX_HEADS * Q_BLOCK, IDX_DIM)
    wi = wi_ref[...]
    chunk = min(SCORE_CHUNK, n_keys)
    for c0 in range(0, n_keys, chunk):
        score = _index_scores(qi_rows, wi, kit_ref[:, c0:c0 + chunk], Q_BLOCK)
        col = c0 + lax.broadcasted_iota(i32, (Q_BLOCK, chunk), 1)
        qpos = s0 + lax.broadcasted_iota(i32, (Q_BLOCK, chunk), 0)
        key_ref[:, c0:c0 + chunk] = jnp.where(col <= qpos, _sort_key(score), INT_MIN)
    bias = jnp.where(_select_topk(key_ref, n_keys, topk), 0.0, -jnp.inf)
    for h in range(N_HEADS):
        g = h // KV_GROUPS
        o = _masked_attention(q_ref[h], kt_ref[g * HEAD_DIM:(g + 1) * HEAD_DIM, 0:n_keys],
                              vt_ref[g * HEAD_DIM:(g + 1) * HEAD_DIM, 0:n_keys], bias)
        o_ref[:, h * HEAD_DIM:(h + 1) * HEAD_DIM] = o


def _attn_prompt_kernel(qi_ref, wi_ref, q_ref, kit_ref, kt_ref, vt_ref, o_ref, key_ref, *, key_class, topk):
    s0 = pl.program_id(1) * Q_BLOCK
    n_classes = kit_ref.shape[1] // key_class
    cls = (s0 + Q_BLOCK - 1) // key_class
    for c in range(n_classes):
        @pl.when(cls == c)
        def _(c=c):
            _attn_prompt_body(qi_ref, wi_ref, q_ref, kit_ref, kt_ref, vt_ref, o_ref, key_ref,
                              s0, (c + 1) * key_class, topk)


def _attn_prompt(qi, wi, q, kit, kt, vt):
    b, _, s, _ = q.shape
    key_class = min(KEY_CLASS, s)
    topk = min(TOPK_MAX, s // 4)
    blk = lambda w: pl.BlockSpec((None, Q_BLOCK, w), lambda bi, i: (bi, i, 0))
    heads = lambda n: pl.BlockSpec((None, n, Q_BLOCK, HEAD_DIM), lambda bi, i: (bi, 0, i, 0))
    whole = lambda r: pl.BlockSpec((None, r, s), lambda bi, i: (bi, 0, 0))
    return pl.pallas_call(
        functools.partial(_attn_prompt_kernel, key_class=key_class, topk=topk),
        grid=(b, s // Q_BLOCK),
        in_specs=[heads(IDX_HEADS), blk(LANES), heads(N_HEADS), whole(IDX_DIM), whole(KV_WIDTH), whole(KV_WIDTH)],
        out_specs=blk(ATT_WIDTH),
        out_shape=jax.ShapeDtypeStruct((b, s, ATT_WIDTH), f32),
        scratch_shapes=[pltpu.VMEM((Q_BLOCK, s), i32)],
        compiler_params=_cparams(("arbitrary", "arbitrary")),
        name="attn_prompt",
    )(qi, wi, q, kit, kt, vt)


def _select_sample_kernel(pt_ref, qi_ref, wi_ref, kin_ref, cki_ref, bias_ref, kit_buf, key_ref, sem,
                          *, n_pages, dec_seq, topk, group):
    step = pl.program_id(0)
    past = n_pages * PAGE_SIZE
    n_keys = past + LANES

    def copy(j):
        s, p = j // n_pages, j % n_pages
        page = pt_ref[(step * group + s) * n_pages + p]
        cols = pl.ds(pl.multiple_of(p * PAGE_SIZE, PAGE_SIZE), PAGE_SIZE)
        return pltpu.make_async_copy(cki_ref.at[page], kit_buf.at[s, :, cols], sem)

    lax.fori_loop(0, group * n_pages, lambda j, c: (copy(j).start(), c)[1], 0)
    kit_buf[:, :, past:n_keys] = kin_ref[...]
    lax.fori_loop(0, group * n_pages, lambda j, c: (copy(j).wait(), c)[1], 0)

    col = lax.broadcasted_iota(i32, (SUBLANES, n_keys), 1)
    tok = lax.broadcasted_iota(i32, (SUBLANES, n_keys), 0)
    allowed = (col < past) | ((col - past <= tok) & (col - past < dec_seq))
    for s in range(group):
        score = _index_scores(qi_ref[s].astype(bf16), wi_ref[s] * IDX_W_SCALE, kit_buf[s].astype(bf16), SUBLANES)
        key_ref[s * SUBLANES:(s + 1) * SUBLANES, :] = jnp.where(allowed, _sort_key(score), INT_MIN)
    bias = jnp.where(_select_topk(key_ref, n_keys, topk), 0.0, -jnp.inf)
    for s in range(group):
        bias_ref[s] = bias[s * SUBLANES:(s + 1) * SUBLANES, :]


def _select_sample(page_table, qi_rows, wi8, kit_new, cache_kit, dec_seq):
    db, n_pages = page_table.shape
    n_keys = n_pages * PAGE_SIZE + LANES
    topk = min(TOPK_MAX, (n_pages * PAGE_SIZE + dec_seq) // 4)
    group = min(SEQ_GROUP, db)
    per = lambda r, w: pl.BlockSpec((group, r, w), lambda i, pt: (i, 0, 0))
    return pl.pallas_call(
        functools.partial(_select_sample_kernel, n_pages=n_pages, dec_seq=dec_seq, topk=topk, group=group),
        grid_spec=pltpu.PrefetchScalarGridSpec(
            num_scalar_prefetch=1, grid=(db // group,),
            in_specs=[per(IDX_HEADS * SUBLANES, IDX_DIM), per(SUBLANES, LANES), per(IDX_DIM, LANES),
                      pl.BlockSpec(memory_space=pl.ANY)],
            out_specs=per(SUBLANES, n_keys),
            scratch_shapes=[pltpu.VMEM((group, IDX_DIM, n_keys), f32), pltpu.VMEM((group * SUBLANES, n_keys), i32),
                            pltpu.SemaphoreType.DMA(())]),
        out_shape=jax.ShapeDtypeStruct((db, SUBLANES, n_keys), f32),
        compiler_params=_cparams(("arbitrary",)),
        name="select_sample",
    )(page_table.reshape(-1), qi_rows, wi8, kit_new, cache_kit)


def _attn_sample_kernel(pt_ref, q_ref, bias_ref, ktn_ref, vtn_ref, ckt_ref, cvt_ref, o_ref, kt_buf, vt_buf, sem,
                        *, n_pages, n_seqs):
    b = pl.program_id(0)
    past = n_pages * PAGE_SIZE
    n_keys = past + LANES

    def pages(seq, slot, wait):
        def body(p, c):
            page = pt_ref[seq * n_pages + p]
            cols = pl.ds(pl.multiple_of(p * PAGE_SIZE, PAGE_SIZE), PAGE_SIZE)
            for src, dst, j in ((ckt_ref, kt_buf, 0), (cvt_ref, vt_buf, 1)):
                cp = pltpu.make_async_copy(src.at[page], dst.at[slot, :, cols], sem.at[j, slot])
                if wait:
                    cp.wait()
                else:
                    cp.start()
            return c

        lax.fori_loop(0, n_pages, body, 0)

    @pl.when(b == 0)
    def _():
        pages(0, 0, False)

    @pl.when(b + 1 < n_seqs)
    def _():
        pages(b + 1, (b + 1) % 2, False)

    slot = b % 2
    kt_buf[slot, :, past:n_keys] = ktn_ref[...]
    vt_buf[slot, :, past:n_keys] = vtn_ref[...]
    pages(b, slot, True)

    bias = bias_ref[...]
    bias2 = jnp.concatenate([bias] * KV_GROUPS, axis=0)
    rows = KV_GROUPS * SUBLANES
    for g in range(N_KV_HEADS):
        kt = kt_buf[slot, g * HEAD_DIM:(g + 1) * HEAD_DIM, :].astype(bf16)
        vt = vt_buf[slot, g * HEAD_DIM:(g + 1) * HEAD_DIM, :].astype(bf16)
        q = (q_ref[g * rows:(g + 1) * rows, :] * HEAD_DIM ** -0.5).astype(bf16)
        o_ref[g * rows:(g + 1) * rows, :] = _masked_attention(q, kt, vt, bias2)


def _attn_sample(page_table, q_rows, bias, kt_new, vt_new, cache_kt, cache_vt):
    db, n_pages = page_table.shape
    n_keys = n_pages * PAGE_SIZE + LANES
    per = lambda r, w: pl.BlockSpec((None, r, w), lambda bi, pt: (bi, 0, 0))
    hbm = pl.BlockSpec(memory_space=pl.ANY)
    return pl.pallas_call(
        functools.partial(_attn_sample_kernel, n_pages=n_pages, n_seqs=db),
        grid_spec=pltpu.PrefetchScalarGridSpec(
            num_scalar_prefetch=1, grid=(db,),
            in_specs=[per(N_HEADS * SUBLANES, HEAD_DIM), per(SUBLANES, n_keys), per(KV_WIDTH, LANES),
                      per(KV_WIDTH, LANES), hbm, hbm],
            out_specs=per(N_HEADS * SUBLANES, HEAD_DIM),
            scratch_shapes=[pltpu.VMEM((2, KV_WIDTH, n_keys), f32), pltpu.VMEM((2, KV_WIDTH, n_keys), f32),
                            pltpu.SemaphoreType.DMA((2, 2))]),
        out_shape=jax.ShapeDtypeStruct((db, N_HEADS * SUBLANES, HEAD_DIM), f32),
        compiler_params=_cparams(("arbitrary",)),
        name="attn_sample",
    )(page_table.reshape(-1), q_rows, bias, kt_new, vt_new, cache_kt, cache_vt)


def _post_body(x_ref, att_ref, conv_ref, sc1_ref, sh1_ref, g1_ref, sc2_ref, sh2_ref,
               wg_ref, wa_ref, wb_ref, wo_ref, wr_ref, br_ref, lng_ref, lnb_ref, cnt_in_ref,
               x1_ref, h2_ref, tope_ref, comb_ref, rank_ref, cnt_ref, *, alpha):
    del cnt_in_ref
    d = x_ref.shape[-1]
    x = x_ref[...]
    hb = (x * (1.0 + sc1_ref[...]) + sh1_ref[...]).astype(bf16)
    ga = jnp.dot(hb, wg_ref[:, 0:d], preferred_element_type=f32)
    gb = jnp.dot(hb, wg_ref[:, d:2 * d], preferred_element_type=f32)
    a = _mm(att_ref[...], wa_ref[...])
    c = _mm(conv_ref[...], wb_ref[...])
    mix = _mm(jax.nn.sigmoid(ga) * a + jax.nn.sigmoid(gb) * c, wo_ref[...])
    x1 = _layer_norm(alpha * x + g1_ref[...] * mix, lng_ref[...], lnb_ref[...])
    x1_ref[...] = x1
    h2 = x1 * (1.0 + sc2_ref[...]) + sh2_ref[...]
    h2_ref[...] = h2
    logits = _mm(h2, wr_ref[...]) + br_ref[...]

    tm = logits.shape[0]
    lane = lax.broadcasted_iota(i32, (tm, LANES), 1)
    tope = jnp.zeros((tm, LANES), i32)
    topv = jnp.full((tm, LANES), -jnp.inf, f32)
    hot = jnp.zeros((tm, LANES), f32)
    picks = []
    work = logits
    for k in range(TOP_K):
        m = jnp.max(work, axis=1, keepdims=True)
        idx = jnp.min(jnp.where(work == m, lane, LANES), axis=1, keepdims=True)
        hit = lane == idx
        tope = jnp.where(lane == k, idx, tope)
        topv = jnp.where(lane == k, m, topv)
        hot = hot + jnp.where(hit, 1.0, 0.0)
        work = jnp.where(hit, -jnp.inf, work)
        picks.append(idx)
    e = jnp.where(lane < TOP_K, jnp.exp(topv - jnp.max(topv, axis=1, keepdims=True)), 0.0)
    comb_ref[...] = e / jnp.sum(e, axis=1, keepdims=True)
    tope_ref[...] = tope

    r_i = lax.broadcasted_iota(i32, (tm, tm), 0)
    c_i = lax.broadcasted_iota(i32, (tm, tm), 1)
    lower = jnp.where(c_i < r_i, 1.0, 0.0)
    before = _mm(lower, hot) + cnt_ref[...]
    rank = jnp.zeros((tm, LANES), f32)
    for k in range(TOP_K):
        rk = jnp.sum(jnp.where(lane == picks[k], before, 0.0), axis=1, keepdims=True)
        rank = jnp.where(lane == k, rk, rank)
    rank_ref[...] = rank.astype(i32)
    cnt_ref[...] = cnt_ref[...] + jnp.sum(hot, axis=0, keepdims=True)


def _post(x, att, conv, mods, weights, cnt_in, alpha, per_row_mods):
    wg, wa, wb, wo, wr, br, lng, lnb = weights
    d = x.shape[-1]
    if per_row_mods:
        n = x.shape[0]
        grid = (1,)
        row = lambda w: pl.BlockSpec((n, w), lambda i: (0, 0))
        mod = row(d)
        const = lambda shp: pl.BlockSpec(shp, lambda i: (0,) * len(shp))
        lead = (n,)
        first = lambda: pl.program_id(0) == 0
        sem = ("arbitrary",)
    else:
        b, s, _ = x.shape
        tm = min(ROW_TILE, s)
        grid = (b, s // tm)
        row = lambda w: pl.BlockSpec((None, tm, w), lambda bi, i: (bi, i, 0))
        mod = pl.BlockSpec((None, 1, d), lambda bi, i: (bi, 0, 0))
        const = lambda shp: pl.BlockSpec(shp, lambda bi, i: (0,) * len(shp))
        lead = (b, s)
        first = lambda: (pl.program_id(0) == 0) & (pl.program_id(1) == 0)
        sem = ("arbitrary", "arbitrary")

    def body(*refs):
        cnt_in_ref, cnt_ref = refs[16], refs[22]

        @pl.when(first())
        def _():
            cnt_ref[...] = cnt_in_ref[...]

        _post_body(*refs, alpha=alpha)

    outs = [jax.ShapeDtypeStruct(lead + (d,), f32), jax.ShapeDtypeStruct(lead + (d,), f32),
            jax.ShapeDtypeStruct(lead + (LANES,), i32), jax.ShapeDtypeStruct(lead + (LANES,), f32),
            jax.ShapeDtypeStruct(lead + (LANES,), i32), jax.ShapeDtypeStruct((1, LANES), f32)]
    return pl.pallas_call(
        body,
        grid=grid,
        in_specs=[row(d), row(ATT_WIDTH), row(512), mod, mod, mod, mod, mod,
                  const(wg.shape), const(wa.shape), const(wb.shape), const(wo.shape), const(wr.shape),
                  const(br.shape), const(lng.shape), const(lnb.shape), const((1, LANES))],
        out_specs=[row(d), row(d), row(LANES), row(LANES), row(LANES), const((1, LANES))],
        out_shape=outs,
        compiler_params=_cparams(sem),
        name="post_rows" if per_row_mods else "post_prompt",
    )(x, att, conv, *mods, wg, wa, wb, wo, wr, br, lng, lnb, cnt_in)


def _dispatch_kernel(pos_ref, pend_ref, hp_ref, hs_ref, xs_ref, zero_ref, sem, zsem, *, n_prompt_steps, n_blocks):
    tm = hp_ref.shape[0]
    step = pl.program_id(0)

    @pl.when(step == 0)
    def _():
        zero_ref[...] = jnp.zeros_like(zero_ref)

        def zero_block(row0, wait):
            dst = xs_ref.at[pl.ds(pl.multiple_of(row0, MOE_ROWS), MOE_ROWS), :]
            cp = pltpu.make_async_copy(zero_ref, dst, zsem)
            if wait:
                cp.wait()
            else:
                cp.start()

        def fill(wait):
            def last_block(e, c):
                end = pend_ref[e]
                begin = jnp.where(e == 0, 0, pend_ref[jnp.maximum(e - 1, 0)])

                @pl.when(end > begin)
                def _():
                    zero_block(end - MOE_ROWS, wait)

                return c

            lax.fori_loop(0, N_EXPERTS, last_block, 0)
            n_used = pend_ref[N_EXPERTS - 1] // MOE_ROWS
            lax.fori_loop(n_used, n_blocks, lambda blk, c: (zero_block(blk * MOE_ROWS, wait), c)[1], 0)

        fill(False)
        fill(True)

    def scatter(src_ref, base):
        def issue(r, c):
            for k in range(TOP_K):
                p = pos_ref[(base + r) * TOP_K + k]
                pltpu.make_async_copy(src_ref.at[pl.ds(r, 1), :], xs_ref.at[pl.ds(p, 1), :], sem).start()
            return c

        def drain(r, c):
            for k in range(TOP_K):
                pltpu.make_async_copy(src_ref.at[pl.ds(0, 1), :], xs_ref.at[pl.ds(0, 1), :], sem).wait()
            return c

        lax.fori_loop(0, src_ref.shape[0], issue, 0)
        lax.fori_loop(0, src_ref.shape[0], drain, 0)

    @pl.when(step < n_prompt_steps)
    def _():
        scatter(hp_ref, step * tm)

    @pl.when(step == n_prompt_steps)
    def _():
        scatter(hs_ref, n_prompt_steps * tm)


def _dispatch(pos, pend, h_prompt, h_sample, n_blocks):
    n, d = h_prompt.shape
    ns = h_sample.shape[0]
    tm = min(ROW_TILE, n)
    steps = n // tm
    return pl.pallas_call(
        functools.partial(_dispatch_kernel, n_prompt_steps=steps, n_blocks=n_blocks),
        grid_spec=pltpu.PrefetchScalarGridSpec(
            num_scalar_prefetch=2, grid=(steps + 1,),
            in_specs=[pl.BlockSpec((tm, d), lambda i, pos, pend: (jnp.minimum(i, steps - 1), 0)),
                      pl.BlockSpec((ns, d), lambda i, pos, pend: (0, 0))],
            out_specs=pl.BlockSpec(memory_space=pl.ANY),
            scratch_shapes=[pltpu.VMEM((MOE_ROWS, d), f32), pltpu.SemaphoreType.DMA(()), pltpu.SemaphoreType.DMA(())]),
        out_shape=jax.ShapeDtypeStruct((n_blocks * MOE_ROWS, d), f32),
        compiler_params=_cparams(("arbitrary",)),
        name="moe_dispatch",
    )(pos, pend, h_prompt, h_sample)


def _expert_kernel(be_ref, nu_ref, xs_ref, wgu_ref, bgu_ref, wd_ref, bd_ref, y_ref, wgu_bf, wd_bf):
    b = pl.program_id(0)

    @pl.when(b < nu_ref[0])
    def _():
        @pl.when((b == 0) | (be_ref[b] != be_ref[jnp.maximum(b - 1, 0)]))
        def _():
            wgu_bf[...] = wgu_ref[...].astype(bf16)
            wd_bf[...] = wd_ref[...].astype(bf16)

        ff = wd_bf.shape[0]
        gu = jnp.dot(xs_ref[...].astype(bf16), wgu_bf[...], preferred_element_type=f32) + bgu_ref[...]
        gate = jnp.minimum(gu[:, 0:ff], SWIGLU_LIMIT)
        up = jnp.clip(gu[:, ff:2 * ff], -SWIGLU_LIMIT, SWIGLU_LIMIT)
        act = gate * jax.nn.sigmoid(SWIGLU_ALPHA * gate) * (up + 1.0)
        y_ref[...] = jnp.dot(act.astype(bf16), wd_bf[...], preferred_element_type=f32) + bd_ref[...]

    @pl.when(b >= nu_ref[0])
    def _():
        y_ref[...] = jnp.zeros_like(y_ref)


def _experts(block_e, n_used, xs, w_gate_up, b_gate_up, w_down, b_down):
    n_rows, d = xs.shape
    n_exp, _, ff2 = w_gate_up.shape
    ff = ff2 // 2
    rows = lambda b, be, nu: (b, 0)
    per_e = lambda b, be, nu: (be[b], 0, 0)
    return pl.pallas_call(
        _expert_kernel,
        grid_spec=pltpu.PrefetchScalarGridSpec(
            num_scalar_prefetch=2, grid=(n_rows // MOE_ROWS,),
            in_specs=[pl.BlockSpec((MOE_ROWS, d), rows),
                      pl.BlockSpec((None, d, ff2), per_e), pl.BlockSpec((None, 1, ff2), per_e),
                      pl.BlockSpec((None, ff, d), per_e), pl.BlockSpec((None, 1, d), per_e)],
            out_specs=pl.BlockSpec((MOE_ROWS, d), rows),
            scratch_shapes=[pltpu.VMEM((d, ff2), bf16), pltpu.VMEM((ff, d), bf16)]),
        out_shape=jax.ShapeDtypeStruct((n_rows, d), f32),
        compiler_params=_cparams(("arbitrary",)),
        name="moe_experts",
    )(block_e, n_used, xs, w_gate_up, b_gate_up.reshape(n_exp, 1, ff2), w_down, b_down.reshape(n_exp, 1, d))


def _combine_kernel(pos_ref, y_ref, x1_ref, comb_ref, g2_ref, lng_ref, lnb_ref, o_ref, buf, sem, *, alpha, n_steps, step_fn):
    tc = x1_ref.shape[0]
    step = step_fn()

    def rows(s, slot, wait):
        base = s * tc

        def body(r, c):
            for k in range(TOP_K):
                p = 0 if wait else pos_ref[(base + r) * TOP_K + k]
                cp = pltpu.make_async_copy(y_ref.at[pl.ds(p, 1), :], buf.at[slot, k, pl.ds(0 if wait else r, 1), :],
                                           sem.at[slot])
                if wait:
                    cp.wait()
                else:
                    cp.start()
            return c

        lax.fori_loop(0, tc, body, 0)

    @pl.when(step == 0)
    def _():
        rows(0, 0, False)

    @pl.when(step + 1 < n_steps)
    def _():
        rows(step + 1, (step + 1) % 2, False)

    slot = step % 2
    rows(step, slot, True)
    comb = comb_ref[...]
    ffn = comb[:, 0:1] * buf[slot, 0]
    for k in range(1, TOP_K):
        ffn = ffn + comb[:, k:k + 1] * buf[slot, k]
    o_ref[...] = _layer_norm(alpha * x1_ref[...] + g2_ref[...] * ffn, lng_ref[...], lnb_ref[...])


def _combine(pos, y, x1, comb, g2, lng, lnb, alpha, per_row_mods):
    d = x1.shape[-1]
    hbm = pl.BlockSpec(memory_space=pl.ANY)
    if per_row_mods:
        n = x1.shape[0]
        tc = min(COMBINE_TILE, n)
        grid = (n // tc,)
        row = lambda w: pl.BlockSpec((tc, w), lambda i, pos: (i, 0))
        mod = row(d)
        const = lambda shp: pl.BlockSpec(shp, lambda i, pos: (0,) * len(shp))
        step_fn = lambda: pl.program_id(0)
        n_steps = n // tc
        sem = ("arbitrary",)
        out_shape = jax.ShapeDtypeStruct((n, d), f32)
    else:
        b, s, _ = x1.shape
        tc = min(COMBINE_TILE, s)
        grid = (b, s // tc)
        row = lambda w: pl.BlockSpec((None, tc, w), lambda bi, i, pos: (bi, i, 0))
        mod = pl.BlockSpec((None, 1, d), lambda bi, i, pos: (bi, 0, 0))
        const = lambda shp: pl.BlockSpec(shp, lambda bi, i, pos: (0,) * len(shp))
        per_b = s // tc
        step_fn = lambda: pl.program_id(0) * per_b + pl.program_id(1)
        n_steps = b * per_b
        sem = ("arbitrary", "arbitrary")
        out_shape = jax.ShapeDtypeStruct((b, s, d), f32)
    return pl.pallas_call(
        functools.partial(_combine_kernel, alpha=alpha, n_steps=n_steps, step_fn=step_fn),
        grid_spec=pltpu.PrefetchScalarGridSpec(
            num_scalar_prefetch=1, grid=grid,
            in_specs=[hbm, row(d), row(LANES), mod, const((1, d)), const((1, d))],
            out_specs=row(d),
            scratch_shapes=[pltpu.VMEM((2, TOP_K, tc, d), f32), pltpu.SemaphoreType.DMA((2,))]),
        out_shape=out_shape,
        compiler_params=_cparams(sem),
        name="moe_combine_rows" if per_row_mods else "moe_combine_prompt",
    )(pos, y, x1, comb, g2, lng, lnb)


def _rope_tables(pos):
    n = pos.shape[0]
    inv_freq = ROPE_THETA ** (-jnp.arange(ROPE_HALF, dtype=f32) * 2.0 / ROPE_DIM)
    ang = pos[:, None] * inv_freq[None, :]
    cos, sin = jnp.cos(ang), jnp.sin(ang)
    rest = HEAD_DIM - ROPE_DIM
    z8, zr = jnp.zeros((n, ROPE_HALF), f32), jnp.zeros((n, rest), f32)
    c = jnp.concatenate([cos, cos, jnp.ones((n, rest), f32)], axis=1)
    s1 = jnp.concatenate([-sin, z8, zr], axis=1)
    s2 = jnp.concatenate([z8, sin, zr], axis=1)
    reps = LANES // HEAD_DIM
    return tuple(jnp.tile(a, (1, reps)) for a in (c, s1, s2))


def _rope_tables_t(pos):
    inv_freq = ROPE_THETA ** (-jnp.arange(ROPE_HALF, dtype=f32) * 2.0 / ROPE_DIM)
    ang = pos[:, None] * inv_freq[None, :]
    return jnp.cos(ang).T, jnp.sin(ang).T


def _pad_cols(a, width, value=0.0):
    return jnp.pad(a, ((0, 0), (0, width - a.shape[1])), constant_values=value)


def _layer(xp, xs, c_prompt, c_sample, cache_k, cache_v, cache_ki, state_conv, page_table, p, alpha):
    b, s, d = xp.shape
    db, t, _ = xs.shape
    ns = db * t
    assert t >= CONV_K - 1 and ns % SUBLANES == 0 and t <= SUBLANES

    n_c = b + db
    c_all = jnp.pad(jnp.concatenate([c_prompt, c_sample], axis=0), ((0, -n_c % SUBLANES), (0, 0)))
    ada = _ada(c_all, p["w_ada"], p["b_ada"])
    mods = [ada[:, j * d:(j + 1) * d] for j in range(6)]
    mp = [m[:b].reshape(b, 1, d) for m in mods]
    ms = [jnp.repeat(m[b:b + db], t, axis=0) for m in mods]

    sizes = (ATT_WIDTH, KV_WIDTH, KV_WIDTH, IDX_HEADS * IDX_DIM, IDX_DIM, IDX_HEADS, d // 2, d // 2, d // 2, d, d)
    cuts = [int(v) for v in np.cumsum(sizes)[:-1]]
    wq, wk, wv, wqi, wki, wwi, wxin, wbg, wcg, wga, wgb = jnp.split(p["w_in"], cuts, axis=1)
    w_proj = jnp.concatenate([wq, wk, wv, wqi, _pad_cols(wki, LANES), _pad_cols(wwi, LANES), wxin, wbg, wcg],
                             axis=1).astype(bf16)
    wg = jnp.concatenate([wga, wgb], axis=1).astype(bf16)
    post_w = (wg, p["w_branch_a"].astype(bf16), p["w_branch_b"].astype(bf16), p["w_o"].astype(bf16),
              _pad_cols(p["w_router"], LANES).astype(bf16), _pad_cols(p["b_router"][None, :], LANES, NEG_BIG),
              p["ln1_g"][None, :], p["ln1_b"][None, :])
    ln2 = (p["ln2_g"][None, :], p["ln2_b"][None, :])

    positions = jnp.arange(s, dtype=f32)
    w_t = jnp.concatenate([wk, wv, wki], axis=1).T.astype(bf16)
    q, qi, wi, kt, vt, kit, kt_b, vt_b, kit_b, conv_p, cst_p = _inproj_prompt(
        xp, mp[1], mp[0], _rope_tables(positions), _rope_tables_t(positions), w_proj, w_t, p["conv_w"])
    att_p = _attn_prompt(qi, wi, q, kit_b, kt_b, vt_b)

    past = page_table.shape[1] * PAGE_SIZE
    tabs_s = _rope_tables(jnp.tile(past + jnp.arange(t, dtype=f32), db))
    zeros = jnp.zeros((db, t, d // 2), f32)
    p1 = zeros.at[:, 0].set(state_conv[:, 1]).reshape(ns, d // 2)
    p2 = zeros.at[:, 0].set(state_conv[:, 0]).at[:, 1].set(state_conv[:, 1]).reshape(ns, d // 2)
    qs, ks, vs, qis, kis, wis, conv_s, u_s = _inproj_sample(xs.reshape(ns, d), ms[1], ms[0], tabs_s, w_proj,
                                                            p["conv_w"], p1, p2, t)
    pad_t = lambda a, rows: jnp.pad(a.reshape(db, t, -1), ((0, 0), (0, rows - t), (0, 0)))
    head_rows = lambda a: pad_t(a, SUBLANES).reshape(db, SUBLANES, -1, HEAD_DIM).transpose(0, 2, 1, 3).reshape(
        db, -1, HEAD_DIM)
    new_cols = lambda a: jnp.pad(a.reshape(db, t, -1).transpose(0, 2, 1), ((0, 0), (0, 0), (0, LANES - t)))
    n_pool = cache_k.shape[0]
    feature_major = lambda c: c.transpose(0, 2, 3, 1).reshape(n_pool, KV_WIDTH, PAGE_SIZE)
    bias_s = _select_sample(page_table, head_rows(qis), pad_t(wis, SUBLANES), new_cols(kis),
                            cache_ki.transpose(0, 2, 1), t)
    att_s = _attn_sample(page_table, head_rows(qs), bias_s, new_cols(ks), new_cols(vs),
                         feature_major(cache_k), feature_major(cache_v))
    att_s = att_s.reshape(db, N_HEADS, SUBLANES, HEAD_DIM)[:, :, :t].transpose(0, 2, 1, 3).reshape(ns, ATT_WIDTH)

    cnt0 = jnp.zeros((1, LANES), f32)
    x1p, h2p, tope_p, comb_p, rank_p, cnt1 = _post(xp, att_p, conv_p, (mp[1], mp[0], mp[2], mp[4], mp[3]),
                                                   post_w, cnt0, alpha, False)
    x1s, h2s, tope_s, comb_s, rank_s, cnt2 = _post(xs.reshape(ns, d), att_s, conv_s, (ms[1], ms[0], ms[2], ms[4], ms[3]),
                                                   post_w, cnt1, alpha, True)

    n_p = b * s
    counts = cnt2[0, :N_EXPERTS].astype(i32)
    padded = (counts + MOE_ROWS - 1) // MOE_ROWS * MOE_ROWS
    pend = jnp.cumsum(padded).astype(i32)
    pstart = pend - padded
    tope = jnp.concatenate([tope_p.reshape(n_p, LANES)[:, :TOP_K], tope_s[:, :TOP_K]], axis=0)
    rank = jnp.concatenate([rank_p.reshape(n_p, LANES)[:, :TOP_K], rank_s[:, :TOP_K]], axis=0)
    pos = (pstart[tope] + rank).astype(i32)
    n_blocks = ((n_p + ns) * TOP_K + N_EXPERTS * (MOE_ROWS - 1)) // MOE_ROWS
    n_used = pend[-1:] // MOE_ROWS
    blk = jnp.minimum(jnp.arange(n_blocks, dtype=i32), n_used[0] - 1)
    block_e = jnp.minimum(jnp.sum(pend[None, :] <= (blk * MOE_ROWS)[:, None], axis=1), N_EXPERTS - 1).astype(i32)
    pos_p, pos_s = pos[:n_p].reshape(-1), pos[n_p:].reshape(-1)

    rows = _dispatch(pos.reshape(-1), pend, h2p.reshape(n_p, d), h2s, n_blocks)
    y = _experts(block_e, n_used, rows, p["w_gate_up"], p["b_gate_up"], p["w_down"], p["b_down"])
    yp = _combine(pos_p, y, x1p, comb_p, mp[5], *ln2, alpha, False)
    ys = _combine(pos_s, y, x1s, comb_s, ms[5], *ln2, alpha, True).reshape(db, t, d)

    token_major = lambda a: a.reshape(b, N_KV_HEADS, HEAD_DIM, s).transpose(0, 3, 1, 2)
    new_p = (token_major(kt), token_major(vt), kit.transpose(0, 2, 1), cst_p)
    new_s = (ks.reshape(db, t, N_KV_HEADS, HEAD_DIM), vs.reshape(db, t, N_KV_HEADS, HEAD_DIM),
             kis.reshape(db, t, IDX_DIM), u_s.reshape(db, t, d // 2)[:, t - (CONV_K - 1):])
    return yp, ys, new_p, new_s


def kernel(x_prompt, x_sample, cache_k, cache_v, cache_idx_k, state_conv, page_table, c_prompt, c_sample,
           w_ada, b_ada, w_in, conv_w, w_branch_a, w_branch_b, w_o, ln1_g, ln1_b, ln2_g, ln2_b,
           w_router, b_router, w_gate_up, b_gate_up, w_down, b_down):
    depth = w_ada.shape[0]
    alpha = float((2 * depth) ** 0.25)
    xp, xs = x_prompt, x_sample
    acc_p, acc_s = [], []
    for l in range(depth):
        p = {"w_ada": w_ada[l], "b_ada": b_ada[l], "w_in": w_in[l], "conv_w": conv_w[l],
             "w_branch_a": w_branch_a[l], "w_branch_b": w_branch_b[l], "w_o": w_o[l],
             "ln1_g": ln1_g[l], "ln1_b": ln1_b[l], "ln2_g": ln2_g[l], "ln2_b": ln2_b[l],
             "w_router": w_router[l], "b_router": b_router[l], "w_gate_up": w_gate_up[l],
             "b_gate_up": b_gate_up[l], "w_down": w_down[l], "b_down": b_down[l]}
        xp, xs, new_p, new_s = _layer(xp, xs, c_prompt, c_sample, cache_k[l], cache_v[l], cache_idx_k[l],
                                      state_conv[l], page_table, p, alpha)
        acc_p.append(new_p)
        acc_s.append(new_s)
    stack = lambda acc, j: jnp.stack([a[j] for a in acc])
    return (xp, xs, stack(acc_p, 0), stack(acc_p, 1), stack(acc_p, 2), stack(acc_p, 3),
            stack(acc_s, 0), stack(acc_s, 1), stack(acc_s, 2), stack(acc_s, 3))
```

```python
import functools

import jax
import jax.numpy as jnp
import numpy as np
from jax import lax
from jax.experimental import pallas as pl
from jax.experimental.pallas import tpu as pltpu

f32 = jnp.float32
bf16 = jnp.bfloat16
i32 = jnp.int32

N_HEADS = 8
HEAD_DIM = 64
N_KV_HEADS = 4
KV_GROUPS = N_HEADS // N_KV_HEADS
ATT_WIDTH = N_HEADS * HEAD_DIM
KV_WIDTH = N_KV_HEADS * HEAD_DIM
ROPE_DIM = HEAD_DIM // 4
ROPE_HALF = ROPE_DIM // 2
ROPE_THETA = 500000.0
IDX_HEADS = 8
IDX_DIM = 64
IDX_W_SCALE = (IDX_HEADS * IDX_DIM) ** -0.5
TOPK_MAX = 256
Q_BLOCK = 128
PAGE_SIZE = 128
CONV_K = 3
N_EXPERTS = 32
TOP_K = 4
SWIGLU_LIMIT = 7.0
SWIGLU_ALPHA = 1.702
LN_EPS = 1e-5

LANES = 128
SUBLANES = 8
VMEM_LIMIT = 56 * 1024 * 1024

ROW_TILE = 256
MOE_ROWS = 256
COMBINE_TILE = 128
KEY_CLASS = 1024
SCORE_CHUNK = 512
SEQ_GROUP = 8
HALF_ROWS = 2 * SUBLANES
BISECT_UNROLL = 4
i16 = jnp.int16
INT_MIN = int(np.iinfo(np.int32).min)
NEG_BIG = -1e30


def _cparams(sem):
    return pltpu.CompilerParams(dimension_semantics=sem, vmem_limit_bytes=VMEM_LIMIT)


def _mm(a, b):
    return jnp.dot(a.astype(bf16), b.astype(bf16), preferred_element_type=f32)


def _mm_nt(a, b):
    return lax.dot_general(a.astype(bf16), b.astype(bf16), (((1,), (1,)), ((), ())), preferred_element_type=f32)


def _layer_norm(y, g, b):
    mu = jnp.mean(y, axis=-1, keepdims=True)
    d = y - mu
    var = jnp.mean(d * d, axis=-1, keepdims=True)
    return d * lax.rsqrt(var + LN_EPS) * g + b


def _ada_kernel(c_ref, w_ref, b_ref, o_ref):
    o_ref[...] = _mm(c_ref[...], w_ref[...]) + b_ref[...]


def _ada(c_all, w_ada, b_ada):
    n, d = c_all.shape
    width = w_ada.shape[1]
    return pl.pallas_call(
        _ada_kernel,
        grid=(width // d,),
        in_specs=[pl.BlockSpec((n, d), lambda j: (0, 0)),
                  pl.BlockSpec((d, d), lambda j: (0, j)),
                  pl.BlockSpec((1, d), lambda j: (0, j))],
        out_specs=pl.BlockSpec((n, d), lambda j: (0, j)),
        out_shape=jax.ShapeDtypeStruct((n, width), f32),
        compiler_params=_cparams(("arbitrary",)),
        name="ada",
    )(c_all, w_ada, b_ada.reshape(1, width))


_SEG = {}
_off = 0
for _name, _w in (("q", ATT_WIDTH), ("k", KV_WIDTH), ("v", KV_WIDTH), ("qi", IDX_HEADS * IDX_DIM),
                  ("ki", LANES), ("wi", LANES), ("xin", 512), ("bg", 512), ("cg", 512)):
    _SEG[_name] = (_off, _off + _w)
    _off += _w
PROJ_WIDTH = _off


def _rope(t, c, s1, s2):
    w = t.shape[1]
    reps = w // LANES
    if reps > 1:
        c, s1, s2 = (jnp.tile(a, (1, reps)) for a in (c, s1, s2))
    return t * c + pltpu.roll(t, w - ROPE_HALF, 1) * s1 + pltpu.roll(t, ROPE_HALF, 1) * s2


def _project(x_ref, sc_ref, sh_ref, cos_ref, s1_ref, s2_ref, w_ref):
    h = x_ref[...] * (1.0 + sc_ref[...]) + sh_ref[...]
    hb = h.astype(bf16)
    c, s1, s2 = cos_ref[...], s1_ref[...], s2_ref[...]

    def seg(name):
        a, b = _SEG[name]
        return jnp.dot(hb, w_ref[:, a:b], preferred_element_type=f32)

    q = _rope(seg("q"), c, s1, s2)
    k = _rope(seg("k"), c, s1, s2)
    v = seg("v")
    qi = _rope(seg("qi"), c, s1, s2)
    ki = _rope(seg("ki"), c, s1, s2)[:, :IDX_DIM]
    wi = seg("wi")
    u = seg("cg") * seg("xin")
    return q, k, v, qi, ki, wi, u, seg("bg")


def _conv_out(u, bg, cw_ref, sh1, sh2):
    y = cw_ref[0:1, :] * sh2 + cw_ref[1:2, :] * sh1
    y = y + cw_ref[2:3, :] * u
    return bg * y


def _rope_rows(t, cos_t, sin_t, n_heads):
    parts = []
    for g in range(n_heads):
        base = g * HEAD_DIM
        r0, r1 = t[base:base + ROPE_HALF], t[base + ROPE_HALF:base + ROPE_DIM]
        parts += [r0 * cos_t - r1 * sin_t, r1 * cos_t + r0 * sin_t, t[base + ROPE_DIM:base + HEAD_DIM]]
    return jnp.concatenate(parts, axis=0)


def _inproj_prompt_kernel(x_ref, sc_ref, sh_ref, cos_ref, s1_ref, s2_ref, cost_ref, sint_ref, w_ref, wt_ref, cw_ref,
                          q_ref, qi_ref, wi_ref, kt_ref, vt_ref, kit_ref, ktb_ref, vtb_ref, kitb_ref,
                          conv_ref, cst_ref, carry_ref):
    @pl.when(pl.program_id(1) == 0)
    def _():
        carry_ref[...] = jnp.zeros_like(carry_ref)

    hb = (x_ref[...] * (1.0 + sc_ref[...]) + sh_ref[...]).astype(bf16)
    c, s1, s2 = cos_ref[...], s1_ref[...], s2_ref[...]

    def seg(name):
        a, b = _SEG[name]
        return jnp.dot(hb, w_ref[:, a:b], preferred_element_type=f32)

    q = _rope(seg("q"), c, s1, s2) * HEAD_DIM ** -0.5
    qi = _rope(seg("qi"), c, s1, s2)
    for h in range(N_HEADS):
        q_ref[h] = q[:, h * HEAD_DIM:(h + 1) * HEAD_DIM].astype(bf16)
    for h in range(IDX_HEADS):
        qi_ref[h] = qi[:, h * IDX_DIM:(h + 1) * IDX_DIM].astype(bf16)
    wi_ref[...] = seg("wi") * IDX_W_SCALE

    t = lax.dot_general(wt_ref[...], hb, (((1,), (1,)), ((), ())), preferred_element_type=f32)
    cos_t, sin_t = cost_ref[...], sint_ref[...]
    kt = _rope_rows(t[0:KV_WIDTH], cos_t, sin_t, N_KV_HEADS)
    vt = t[KV_WIDTH:2 * KV_WIDTH]
    kit = _rope_rows(t[2 * KV_WIDTH:2 * KV_WIDTH + IDX_DIM], cos_t, sin_t, 1)
    kt_ref[...], vt_ref[...], kit_ref[...] = kt, vt, kit
    ktb_ref[...], vtb_ref[...], kitb_ref[...] = kt.astype(bf16), vt.astype(bf16), kit.astype(bf16)

    u = seg("cg") * seg("xin")
    bg = seg("bg")
    tm = u.shape[0]
    rows = lax.broadcasted_iota(i32, u.shape, 0)
    c0, c1 = carry_ref[0:1, :], carry_ref[1:2, :]
    sh1 = jnp.where(rows == 0, c1, pltpu.roll(u, 1, 0))
    sh2 = jnp.where(rows == 0, c0, jnp.where(rows == 1, c1, pltpu.roll(u, 2, 0)))
    conv_ref[...] = _conv_out(u, bg, cw_ref, sh1, sh2)
    tail = u[tm - 2:tm, :]
    carry_ref[0:2, :] = tail
    cst_ref[...] = tail


def _inproj_sample_kernel(x_ref, sc_ref, sh_ref, cos_ref, s1_ref, s2_ref, w_ref, cw_ref, p1_ref, p2_ref,
                          q_ref, k_ref, v_ref, qi_ref, ki_ref, wi_ref, conv_ref, u_ref, *, dec_seq):
    q, k, v, qi, ki, wi, u, bg = _project(x_ref, sc_ref, sh_ref, cos_ref, s1_ref, s2_ref, w_ref)
    q_ref[...], k_ref[...], v_ref[...], qi_ref[...], ki_ref[...], wi_ref[...] = q, k, v, qi, ki, wi
    t = lax.broadcasted_iota(i32, u.shape, 0) % dec_seq
    sh1 = jnp.where(t == 0, p1_ref[...], pltpu.roll(u, 1, 0))
    sh2 = jnp.where(t < 2, p2_ref[...], pltpu.roll(u, 2, 0))
    conv_ref[...] = _conv_out(u, bg, cw_ref, sh1, sh2)
    u_ref[...] = u


def _inproj_prompt(x, sc1, sh1, tabs, tabs_t, w_proj, w_t, conv_w):
    b, s, d = x.shape
    tm = min(ROW_TILE, s)
    row = lambda w: pl.BlockSpec((None, tm, w), lambda bi, i: (bi, i, 0))
    heads = lambda n: pl.BlockSpec((None, n, tm, HEAD_DIM), lambda bi, i: (bi, 0, i, 0))
    col = lambda r: pl.BlockSpec((None, r, tm), lambda bi, i: (bi, 0, i))
    per_b = pl.BlockSpec((None, 1, d), lambda bi, i: (bi, 0, 0))
    tab = pl.BlockSpec((tm, LANES), lambda bi, i: (i, 0))
    tab_t = pl.BlockSpec((ROPE_HALF, tm), lambda bi, i: (0, i))
    const = lambda a: pl.BlockSpec(a.shape, lambda bi, i: (0,) * a.ndim)
    feat = (KV_WIDTH, KV_WIDTH, IDX_DIM)
    return pl.pallas_call(
        _inproj_prompt_kernel,
        grid=(b, s // tm),
        in_specs=[row(d), per_b, per_b, tab, tab, tab, tab_t, tab_t, const(w_proj), const(w_t), const(conv_w)],
        out_specs=[heads(N_HEADS), heads(IDX_HEADS), row(LANES)] + [col(r) for r in feat] + [col(r) for r in feat]
        + [row(512), pl.BlockSpec((None, CONV_K - 1, 512), lambda bi, i: (bi, 0, 0))],
        out_shape=[jax.ShapeDtypeStruct((b, N_HEADS, s, HEAD_DIM), bf16),
                   jax.ShapeDtypeStruct((b, IDX_HEADS, s, IDX_DIM), bf16),
                   jax.ShapeDtypeStruct((b, s, LANES), f32)]
        + [jax.ShapeDtypeStruct((b, r, s), f32) for r in feat]
        + [jax.ShapeDtypeStruct((b, r, s), bf16) for r in feat]
        + [jax.ShapeDtypeStruct((b, s, 512), f32), jax.ShapeDtypeStruct((b, CONV_K - 1, 512), f32)],
        scratch_shapes=[pltpu.VMEM((SUBLANES, 512), f32)],
        compiler_params=_cparams(("arbitrary", "arbitrary")),
        name="inproj_prompt",
    )(x, sc1, sh1, *tabs, *tabs_t, w_proj, w_t, conv_w)


def _inproj_sample(x, sc1, sh1, tabs, w_proj, conv_w, p1, p2, dec_seq):
    n, d = x.shape
    widths = (ATT_WIDTH, KV_WIDTH, KV_WIDTH, IDX_HEADS * IDX_DIM, IDX_DIM, LANES, 512, 512)
    full = lambda r, w: pl.BlockSpec((r, w), lambda i: (0, 0))
    return pl.pallas_call(
        functools.partial(_inproj_sample_kernel, dec_seq=dec_seq),
        grid=(1,),
        in_specs=[full(n, d), full(n, d), full(n, d), full(n, LANES), full(n, LANES), full(n, LANES),
                  full(d, PROJ_WIDTH), full(CONV_K, 512), full(n, 512), full(n, 512)],
        out_specs=[full(n, w) for w in widths],
        out_shape=[jax.ShapeDtypeStruct((n, w), f32) for w in widths],
        compiler_params=_cparams(("arbitrary",)),
        name="inproj_sample",
    )(x, sc1, sh1, *tabs, w_proj, conv_w, p1, p2)


def _sort_key(score):
    bits = pltpu.bitcast(jnp.where(score == 0.0, 0.0, score), i32)
    return bits ^ ((bits >> 31) & jnp.int32(0x7FFFFFFF))


def _bisect16(ref, n_cols, need):
    n_groups = ref.shape[0] // HALF_ROWS
    one, zero = jnp.ones((), i16), jnp.zeros((), i16)

    def refine(bit, ts):
        out = []
        for g in range(n_groups):
            rows = slice(g * HALF_ROWS, (g + 1) * HALF_ROWS)
            cand = ts[g] + jnp.left_shift(jnp.int32(1), bit)
            cand16 = cand.astype(i16)
            acc = jnp.zeros((HALF_ROWS, LANES), i16)
            for c in range(n_cols // LANES):
                acc = acc + jnp.where(ref[rows, c * LANES:(c + 1) * LANES] >= cand16, one, zero)
            cnt = jnp.sum(acc.astype(f32), axis=1, keepdims=True)
            out.append(jnp.where(cnt >= need[rows], cand, ts[g]))
        return tuple(out)

    def outer(i, ts):
        for j in range(BISECT_UNROLL):
            ts = refine(15 - (i * BISECT_UNROLL + j), ts)
        return ts

    ts = lax.fori_loop(0, 16 // BISECT_UNROLL, outer,
                       tuple(jnp.full((HALF_ROWS, 1), -32768, i32) for _ in range(n_groups)))
    return jnp.concatenate(ts, axis=0)


def _select_topk(key_ref, hi_ref, lo_ref, n_cols, topk):
    n_rows = key_ref.shape[0]
    kf = float(topk)

    def count(pred):
        return jnp.sum(jnp.where(pred, 1.0, 0.0), axis=1, keepdims=True)

    for c0 in range(0, n_cols, SCORE_CHUNK):
        cols = slice(c0, min(c0 + SCORE_CHUNK, n_cols))
        k = key_ref[:, cols]
        hi_ref[:, cols] = (k >> 16).astype(i16)
        lo_ref[:, cols] = ((k & 0xFFFF) - 32768).astype(i16)
    t_hi = _bisect16(hi_ref, n_cols, jnp.full((n_rows, 1), kf, f32))
    t_hi16 = t_hi.astype(i16)
    acc = jnp.zeros((n_rows, LANES), i16)
    for c in range(n_cols // LANES):
        cols = slice(c * LANES, (c + 1) * LANES)
        h = hi_ref[:, cols]
        acc = acc + jnp.where(h > t_hi16, jnp.ones((), i16), jnp.zeros((), i16))
        lo_ref[:, cols] = jnp.where(h == t_hi16, lo_ref[:, cols], jnp.full((), -32768, i16))
    above = jnp.sum(acc.astype(f32), axis=1, keepdims=True)
    t_lo = _bisect16(lo_ref, n_cols, kf - above)
    t = jnp.left_shift(t_hi, 16) + (t_lo + 32768)
    keys = key_ref[:, 0:n_cols]
    n_ge = count(keys >= t)
    col = lax.broadcasted_iota(i32, (n_rows, n_cols), 1)
    n_bits = int(n_cols).bit_length()

    def tie_search(_):
        need = kf - count(key_ref[:, 0:n_cols] > t)

        def tstep(i, j):
            cand = j + jnp.left_shift(jnp.int32(1), n_bits - 1 - i)
            below = count((key_ref[:, 0:n_cols] == t) & (col < cand))
            return jnp.where(below < need, cand, j)

        return lax.fori_loop(0, n_bits, tstep, jnp.zeros((n_rows, 1), i32))

    jmax = lax.cond(jnp.max(n_ge) > kf, tie_search, lambda _: jnp.full((n_rows, 1), n_cols, i32), 0)
    return (keys > INT_MIN) & ((keys > t) | ((keys == t) & (col <= jmax)))


def _masked_attention(q, kt, vt, bias):
    logits = jnp.dot(q, kt, preferred_element_type=f32) + bias
    m = jnp.max(logits, axis=1, keepdims=True)
    p = jnp.exp(logits - m)
    den = jnp.sum(p, axis=1, keepdims=True)
    o = lax.dot_general(p.astype(bf16), vt, (((1,), (1,)), ((), ())), preferred_element_type=f32)
    return o / den


def _index_scores(qi_rows, wi, kit, n_heads_rows):
    dots = jnp.dot(qi_rows, kit, preferred_element_type=f32)
    r = n_heads_rows
    score = wi[:, 0:1] * jnp.maximum(dots[0:r], 0.0)
    for h in range(1, IDX_HEADS):
        score = score + wi[:, h:h + 1] * jnp.maximum(dots[h * r:(h + 1) * r], 0.0)
    return score


def _attn_prompt_body(qi_ref, wi_ref, q_ref, kit_ref, kt_ref, vt_ref, o_ref, key_ref, hi_ref, lo_ref, s0, n_keys, topk):
    qi_rows = qi_ref[...].reshape(IDX_HEADS * Q_BLOCK, IDX_DIM)
    wi = wi_ref[...]
    chunk = min(SCORE_CHUNK, n_keys)
    for c0 in range(0, n_keys, chunk):
        score = _index_scores(qi_rows, wi, kit_ref[:, c0:c0 + chunk], Q_BLOCK)
        col = c0 + lax.broadcasted_iota(i32, (Q_BLOCK, chunk), 1)
        qpos = s0 + lax.broadcasted_iota(i32, (Q_BLOCK, chunk), 0)
        key_ref[:, c0:c0 + chunk] = jnp.where(col <= qpos, _sort_key(score), INT_MIN)
    bias = jnp.where(_select_topk(key_ref, hi_ref, lo_ref, n_keys, topk), 0.0, -jnp.inf)
    for h in range(N_HEADS):
        g = h // KV_GROUPS
        o = _masked_attention(q_ref[h], kt_ref[g * HEAD_DIM:(g + 1) * HEAD_DIM, 0:n_keys],
                              vt_ref[g * HEAD_DIM:(g + 1) * HEAD_DIM, 0:n_keys], bias)
        o_ref[:, h * HEAD_DIM:(h + 1) * HEAD_DIM] = o


def _attn_prompt_kernel(qi_ref, wi_ref, q_ref, kit_ref, kt_ref, vt_ref, o_ref, key_ref, hi_ref, lo_ref,
                        *, key_class, topk):
    s0 = pl.program_id(1) * Q_BLOCK
    n_classes = kit_ref.shape[1] // key_class
    cls = (s0 + Q_BLOCK - 1) // key_class
    for c in range(n_classes):
        @pl.when(cls == c)
        def _(c=c):
            _attn_prompt_body(qi_ref, wi_ref, q_ref, kit_ref, kt_ref, vt_ref, o_ref, key_ref, hi_ref, lo_ref,
                              s0, (c + 1) * key_class, topk)


def _attn_prompt(qi, wi, q, kit, kt, vt):
    b, _, s, _ = q.shape
    key_class = min(KEY_CLASS, s)
    topk = min(TOPK_MAX, s // 4)
    blk = lambda w: pl.BlockSpec((None, Q_BLOCK, w), lambda bi, i: (bi, i, 0))
    heads = lambda n: pl.BlockSpec((None, n, Q_BLOCK, HEAD_DIM), lambda bi, i: (bi, 0, i, 0))
    whole = lambda r: pl.BlockSpec((None, r, s), lambda bi, i: (bi, 0, 0))
    return pl.pallas_call(
        functools.partial(_attn_prompt_kernel, key_class=key_class, topk=topk),
        grid=(b, s // Q_BLOCK),
        in_specs=[heads(IDX_HEADS), blk(LANES), heads(N_HEADS), whole(IDX_DIM), whole(KV_WIDTH), whole(KV_WIDTH)],
        out_specs=blk(ATT_WIDTH),
        out_shape=jax.ShapeDtypeStruct((b, s, ATT_WIDTH), f32),
        scratch_shapes=[pltpu.VMEM((Q_BLOCK, s), i32), pltpu.VMEM((Q_BLOCK, s), i16), pltpu.VMEM((Q_BLOCK, s), i16)],
        compiler_params=_cparams(("arbitrary", "arbitrary")),
        name="attn_prompt",
    )(qi, wi, q, kit, kt, vt)


def _select_sample_kernel(pt_ref, qi_ref, wi_ref, kin_ref, cki_ref, bias_ref, kit_buf, key_ref, hi_ref, lo_ref, sem,
                          *, n_pages, dec_seq, topk, group):
    step = pl.program_id(0)
    past = n_pages * PAGE_SIZE
    n_keys = past + LANES

    def copy(j):
        s, p = j // n_pages, j % n_pages
        page = pt_ref[(step * group + s) * n_pages + p]
        cols = pl.ds(pl.multiple_of(p * PAGE_SIZE, PAGE_SIZE), PAGE_SIZE)
        return pltpu.make_async_copy(cki_ref.at[page], kit_buf.at[s, :, cols], sem)

    lax.fori_loop(0, group * n_pages, lambda j, c: (copy(j).start(), c)[1], 0)
    kit_buf[:, :, past:n_keys] = kin_ref[...]
    lax.fori_loop(0, group * n_pages, lambda j, c: (copy(j).wait(), c)[1], 0)

    col = lax.broadcasted_iota(i32, (SUBLANES, n_keys), 1)
    tok = lax.broadcasted_iota(i32, (SUBLANES, n_keys), 0)
    allowed = (col < past) | ((col - past <= tok) & (col - past < dec_seq))
    for s in range(group):
        score = _index_scores(qi_ref[s].astype(bf16), wi_ref[s] * IDX_W_SCALE, kit_buf[s].astype(bf16), SUBLANES)
        key_ref[s * SUBLANES:(s + 1) * SUBLANES, :] = jnp.where(allowed, _sort_key(score), INT_MIN)
    bias = jnp.where(_select_topk(key_ref, hi_ref, lo_ref, n_keys, topk), 0.0, -jnp.inf)
    for s in range(group):
        bias_ref[s] = bias[s * SUBLANES:(s + 1) * SUBLANES, :]


def _select_sample(page_table, qi_rows, wi8, kit_new, cache_kit, dec_seq):
    db, n_pages = page_table.shape
    n_keys = n_pages * PAGE_SIZE + LANES
    topk = min(TOPK_MAX, (n_pages * PAGE_SIZE + dec_seq) // 4)
    group = min(SEQ_GROUP, db)
    per = lambda r, w: pl.BlockSpec((group, r, w), lambda i, pt: (i, 0, 0))
    return pl.pallas_call(
        functools.partial(_select_sample_kernel, n_pages=n_pages, dec_seq=dec_seq, topk=topk, group=group),
        grid_spec=pltpu.PrefetchScalarGridSpec(
            num_scalar_prefetch=1, grid=(db // group,),
            in_specs=[per(IDX_HEADS * SUBLANES, IDX_DIM), per(SUBLANES, LANES), per(IDX_DIM, LANES),
                      pl.BlockSpec(memory_space=pl.ANY)],
            out_specs=per(SUBLANES, n_keys),
            scratch_shapes=[pltpu.VMEM((group, IDX_DIM, n_keys), f32), pltpu.VMEM((group * SUBLANES, n_keys), i32),
                            pltpu.VMEM((group * SUBLANES, n_keys), i16), pltpu.VMEM((group * SUBLANES, n_keys), i16),
                            pltpu.SemaphoreType.DMA(())]),
        out_shape=jax.ShapeDtypeStruct((db, SUBLANES, n_keys), f32),
        compiler_params=_cparams(("arbitrary",)),
        name="select_sample",
    )(page_table.reshape(-1), qi_rows, wi8, kit_new, cache_kit)


def _attn_sample_kernel(pt_ref, q_ref, bias_ref, ktn_ref, vtn_ref, ckt_ref, cvt_ref, o_ref, kt_buf, vt_buf, sem,
                        *, n_pages, n_seqs):
    b = pl.program_id(0)
    past = n_pages * PAGE_SIZE
    n_keys = past + LANES

    def pages(seq, slot, wait):
        def body(p, c):
            page = pt_ref[seq * n_pages + p]
            cols = pl.ds(pl.multiple_of(p * PAGE_SIZE, PAGE_SIZE), PAGE_SIZE)
            for src, dst, j in ((ckt_ref, kt_buf, 0), (cvt_ref, vt_buf, 1)):
                cp = pltpu.make_async_copy(src.at[page], dst.at[slot, :, cols], sem.at[j, slot])
                if wait:
                    cp.wait()
                else:
                    cp.start()
            return c

        lax.fori_loop(0, n_pages, body, 0)

    @pl.when(b == 0)
    def _():
        pages(0, 0, False)

    @pl.when(b + 1 < n_seqs)
    def _():
        pages(b + 1, (b + 1) % 2, False)

    slot = b % 2
    kt_buf[slot, :, past:n_keys] = ktn_ref[...]
    vt_buf[slot, :, past:n_keys] = vtn_ref[...]
    pages(b, slot, True)

    bias = bias_ref[...]
    bias2 = jnp.concatenate([bias] * KV_GROUPS, axis=0)
    rows = KV_GROUPS * SUBLANES
    for g in range(N_KV_HEADS):
        kt = kt_buf[slot, g * HEAD_DIM:(g + 1) * HEAD_DIM, :].astype(bf16)
        vt = vt_buf[slot, g * HEAD_DIM:(g + 1) * HEAD_DIM, :].astype(bf16)
        q = (q_ref[g * rows:(g + 1) * rows, :] * HEAD_DIM ** -0.5).astype(bf16)
        o_ref[g * rows:(g + 1) * rows, :] = _masked_attention(q, kt, vt, bias2)


def _attn_sample(page_table, q_rows, bias, kt_new, vt_new, cache_kt, cache_vt):
    db, n_pages = page_table.shape
    n_keys = n_pages * PAGE_SIZE + LANES
    per = lambda r, w: pl.BlockSpec((None, r, w), lambda bi, pt: (bi, 0, 0))
    hbm = pl.BlockSpec(memory_space=pl.ANY)
    return pl.pallas_call(
        functools.partial(_attn_sample_kernel, n_pages=n_pages, n_seqs=db),
        grid_spec=pltpu.PrefetchScalarGridSpec(
            num_scalar_prefetch=1, grid=(db,),
            in_specs=[per(N_HEADS * SUBLANES, HEAD_DIM), per(SUBLANES, n_keys), per(KV_WIDTH, LANES),
                      per(KV_WIDTH, LANES), hbm, hbm],
            out_specs=per(N_HEADS * SUBLANES, HEAD_DIM),
            scratch_shapes=[pltpu.VMEM((2, KV_WIDTH, n_keys), f32), pltpu.VMEM((2, KV_WIDTH, n_keys), f32),
                            pltpu.SemaphoreType.DMA((2, 2))]),
        out_shape=jax.ShapeDtypeStruct((db, N_HEADS * SUBLANES, HEAD_DIM), f32),
        compiler_params=_cparams(("arbitrary",)),
        name="attn_sample",
    )(page_table.reshape(-1), q_rows, bias, kt_new, vt_new, cache_kt, cache_vt)


def _post_body(x_ref, att_ref, conv_ref, sc1_ref, sh1_ref, g1_ref, sc2_ref, sh2_ref,
               wg_ref, wa_ref, wb_ref, wo_ref, wr_ref, br_ref, lng_ref, lnb_ref, cnt_in_ref,
               x1_ref, h2_ref, tope_ref, comb_ref, rank_ref, cnt_ref, *, alpha):
    del cnt_in_ref
    d = x_ref.shape[-1]
    x = x_ref[...]
    hb = (x * (1.0 + sc1_ref[...]) + sh1_ref[...]).astype(bf16)
    ga = jnp.dot(hb, wg_ref[:, 0:d], preferred_element_type=f32)
    gb = jnp.dot(hb, wg_ref[:, d:2 * d], preferred_element_type=f32)
    a = _mm(att_ref[...], wa_ref[...])
    c = _mm(conv_ref[...], wb_ref[...])
    mix = _mm(jax.nn.sigmoid(ga) * a + jax.nn.sigmoid(gb) * c, wo_ref[...])
    x1 = _layer_norm(alpha * x + g1_ref[...] * mix, lng_ref[...], lnb_ref[...])
    x1_ref[...] = x1
    h2 = x1 * (1.0 + sc2_ref[...]) + sh2_ref[...]
    h2_ref[...] = h2
    logits = _mm(h2, wr_ref[...]) + br_ref[...]

    tm = logits.shape[0]
    lane = lax.broadcasted_iota(i32, (tm, LANES), 1)
    tope = jnp.zeros((tm, LANES), i32)
    topv = jnp.full((tm, LANES), -jnp.inf, f32)
    hot = jnp.zeros((tm, LANES), f32)
    picks = []
    work = logits
    for k in range(TOP_K):
        m = jnp.max(work, axis=1, keepdims=True)
        idx = jnp.min(jnp.where(work == m, lane, LANES), axis=1, keepdims=True)
        hit = lane == idx
        tope = jnp.where(lane == k, idx, tope)
        topv = jnp.where(lane == k, m, topv)
        hot = hot + jnp.where(hit, 1.0, 0.0)
        work = jnp.where(hit, -jnp.inf, work)
        picks.append(idx)
    e = jnp.where(lane < TOP_K, jnp.exp(topv - jnp.max(topv, axis=1, keepdims=True)), 0.0)
    comb_ref[...] = e / jnp.sum(e, axis=1, keepdims=True)
    tope_ref[...] = tope

    r_i = lax.broadcasted_iota(i32, (tm, tm), 0)
    c_i = lax.broadcasted_iota(i32, (tm, tm), 1)
    lower = jnp.where(c_i < r_i, 1.0, 0.0)
    before = _mm(lower, hot) + cnt_ref[...]
    rank = jnp.zeros((tm, LANES), f32)
    for k in range(TOP_K):
        rk = jnp.sum(jnp.where(lane == picks[k], before, 0.0), axis=1, keepdims=True)
        rank = jnp.where(lane == k, rk, rank)
    rank_ref[...] = rank.astype(i32)
    cnt_ref[...] = cnt_ref[...] + jnp.sum(hot, axis=0, keepdims=True)


def _post(x, att, conv, mods, weights, cnt_in, alpha, per_row_mods):
    wg, wa, wb, wo, wr, br, lng, lnb = weights
    d = x.shape[-1]
    if per_row_mods:
        n = x.shape[0]
        grid = (1,)
        row = lambda w: pl.BlockSpec((n, w), lambda i: (0, 0))
        mod = row(d)
        const = lambda shp: pl.BlockSpec(shp, lambda i: (0,) * len(shp))
        lead = (n,)
        first = lambda: pl.program_id(0) == 0
        sem = ("arbitrary",)
    else:
        b, s, _ = x.shape
        tm = min(ROW_TILE, s)
        grid = (b, s // tm)
        row = lambda w: pl.BlockSpec((None, tm, w), lambda bi, i: (bi, i, 0))
        mod = pl.BlockSpec((None, 1, d), lambda bi, i: (bi, 0, 0))
        const = lambda shp: pl.BlockSpec(shp, lambda bi, i: (0,) * len(shp))
        lead = (b, s)
        first = lambda: (pl.program_id(0) == 0) & (pl.program_id(1) == 0)
        sem = ("arbitrary", "arbitrary")

    def body(*refs):
        cnt_in_ref, cnt_ref = refs[16], refs[22]

        @pl.when(first())
        def _():
            cnt_ref[...] = cnt_in_ref[...]

        _post_body(*refs, alpha=alpha)

    outs = [jax.ShapeDtypeStruct(lead + (d,), f32), jax.ShapeDtypeStruct(lead + (d,), f32),
            jax.ShapeDtypeStruct(lead + (LANES,), i32), jax.ShapeDtypeStruct(lead + (LANES,), f32),
            jax.ShapeDtypeStruct(lead + (LANES,), i32), jax.ShapeDtypeStruct((1, LANES), f32)]
    return pl.pallas_call(
        body,
        grid=grid,
        in_specs=[row(d), row(ATT_WIDTH), row(512), mod, mod, mod, mod, mod,
                  const(wg.shape), const(wa.shape), const(wb.shape), const(wo.shape), const(wr.shape),
                  const(br.shape), const(lng.shape), const(lnb.shape), const((1, LANES))],
        out_specs=[row(d), row(d), row(LANES), row(LANES), row(LANES), const((1, LANES))],
        out_shape=outs,
        compiler_params=_cparams(sem),
        name="post_rows" if per_row_mods else "post_prompt",
    )(x, att, conv, *mods, wg, wa, wb, wo, wr, br, lng, lnb, cnt_in)


def _dispatch_kernel(pos_ref, pend_ref, hp_ref, hs_ref, xs_ref, zero_ref, sem, zsem, *, n_prompt_steps, n_blocks):
    tm = hp_ref.shape[0]
    step = pl.program_id(0)

    @pl.when(step == 0)
    def _():
        zero_ref[...] = jnp.zeros_like(zero_ref)

        def zero_block(row0, wait):
            dst = xs_ref.at[pl.ds(pl.multiple_of(row0, MOE_ROWS), MOE_ROWS), :]
            cp = pltpu.make_async_copy(zero_ref, dst, zsem)
            if wait:
                cp.wait()
            else:
                cp.start()

        def fill(wait):
            def last_block(e, c):
                end = pend_ref[e]
                begin = jnp.where(e == 0, 0, pend_ref[jnp.maximum(e - 1, 0)])

                @pl.when(end > begin)
                def _():
                    zero_block(end - MOE_ROWS, wait)

                return c

            lax.fori_loop(0, N_EXPERTS, last_block, 0)
            n_used = pend_ref[N_EXPERTS - 1] // MOE_ROWS
            lax.fori_loop(n_used, n_blocks, lambda blk, c: (zero_block(blk * MOE_ROWS, wait), c)[1], 0)

        fill(False)
        fill(True)

    def scatter(src_ref, base):
        def issue(r, c):
            for k in range(TOP_K):
                p = pos_ref[(base + r) * TOP_K + k]
                pltpu.make_async_copy(src_ref.at[pl.ds(r, 1), :], xs_ref.at[pl.ds(p, 1), :], sem).start()
            return c

        def drain(r, c):
            for k in range(TOP_K):
                pltpu.make_async_copy(src_ref.at[pl.ds(0, 1), :], xs_ref.at[pl.ds(0, 1), :], sem).wait()
            return c

        lax.fori_loop(0, src_ref.shape[0], issue, 0)
        lax.fori_loop(0, src_ref.shape[0], drain, 0)

    @pl.when(step < n_prompt_steps)
    def _():
        scatter(hp_ref, step * tm)

    @pl.when(step == n_prompt_steps)
    def _():
        scatter(hs_ref, n_prompt_steps * tm)


def _dispatch(pos, pend, h_prompt, h_sample, n_blocks):
    n, d = h_prompt.shape
    ns = h_sample.shape[0]
    tm = min(ROW_TILE, n)
    steps = n // tm
    return pl.pallas_call(
        functools.partial(_dispatch_kernel, n_prompt_steps=steps, n_blocks=n_blocks),
        grid_spec=pltpu.PrefetchScalarGridSpec(
            num_scalar_prefetch=2, grid=(steps + 1,),
            in_specs=[pl.BlockSpec((tm, d), lambda i, pos, pend: (jnp.minimum(i, steps - 1), 0)),
                      pl.BlockSpec((ns, d), lambda i, pos, pend: (0, 0))],
            out_specs=pl.BlockSpec(memory_space=pl.ANY),
            scratch_shapes=[pltpu.VMEM((MOE_ROWS, d), f32), pltpu.SemaphoreType.DMA(()), pltpu.SemaphoreType.DMA(())]),
        out_shape=jax.ShapeDtypeStruct((n_blocks * MOE_ROWS, d), f32),
        compiler_params=_cparams(("arbitrary",)),
        name="moe_dispatch",
    )(pos, pend, h_prompt, h_sample)


def _expert_kernel(be_ref, nu_ref, xs_ref, wgu_ref, bgu_ref, wd_ref, bd_ref, y_ref, wgu_bf, wd_bf):
    b = pl.program_id(0)

    @pl.when(b < nu_ref[0])
    def _():
        @pl.when((b == 0) | (be_ref[b] != be_ref[jnp.maximum(b - 1, 0)]))
        def _():
            wgu_bf[...] = wgu_ref[...].astype(bf16)
            wd_bf[...] = wd_ref[...].astype(bf16)

        ff = wd_bf.shape[0]
        gu = jnp.dot(xs_ref[...].astype(bf16), wgu_bf[...], preferred_element_type=f32) + bgu_ref[...]
        gate = jnp.minimum(gu[:, 0:ff], SWIGLU_LIMIT)
        up = jnp.clip(gu[:, ff:2 * ff], -SWIGLU_LIMIT, SWIGLU_LIMIT)
        act = gate * jax.nn.sigmoid(SWIGLU_ALPHA * gate) * (up + 1.0)
        y_ref[...] = jnp.dot(act.astype(bf16), wd_bf[...], preferred_element_type=f32) + bd_ref[...]

    @pl.when(b >= nu_ref[0])
    def _():
        y_ref[...] = jnp.zeros_like(y_ref)


def _experts(block_e, n_used, xs, w_gate_up, b_gate_up, w_down, b_down):
    n_rows, d = xs.shape
    n_exp, _, ff2 = w_gate_up.shape
    ff = ff2 // 2
    rows = lambda b, be, nu: (b, 0)
    per_e = lambda b, be, nu: (be[b], 0, 0)
    return pl.pallas_call(
        _expert_kernel,
        grid_spec=pltpu.PrefetchScalarGridSpec(
            num_scalar_prefetch=2, grid=(n_rows // MOE_ROWS,),
            in_specs=[pl.BlockSpec((MOE_ROWS, d), rows),
                      pl.BlockSpec((None, d, ff2), per_e), pl.BlockSpec((None, 1, ff2), per_e),
                      pl.BlockSpec((None, ff, d), per_e), pl.BlockSpec((None, 1, d), per_e)],
            out_specs=pl.BlockSpec((MOE_ROWS, d), rows),
            scratch_shapes=[pltpu.VMEM((d, ff2), bf16), pltpu.VMEM((ff, d), bf16)]),
        out_shape=jax.ShapeDtypeStruct((n_rows, d), f32),
        compiler_params=_cparams(("arbitrary",)),
        name="moe_experts",
    )(block_e, n_used, xs, w_gate_up, b_gate_up.reshape(n_exp, 1, ff2), w_down, b_down.reshape(n_exp, 1, d))


def _combine_kernel(pos_ref, y_ref, x1_ref, comb_ref, g2_ref, lng_ref, lnb_ref, o_ref, buf, sem, *, alpha, n_steps, step_fn):
    tc = x1_ref.shape[0]
    step = step_fn()

    def rows(s, slot, wait):
        base = s * tc

        def body(r, c):
            for k in range(TOP_K):
                p = 0 if wait else pos_ref[(base + r) * TOP_K + k]
                cp = pltpu.make_async_copy(y_ref.at[pl.ds(p, 1), :], buf.at[slot, k, pl.ds(0 if wait else r, 1), :],
                                           sem.at[slot])
                if wait:
                    cp.wait()
                else:
                    cp.start()
            return c

        lax.fori_loop(0, tc, body, 0)

    @pl.when(step == 0)
    def _():
        rows(0, 0, False)

    for sl in range(2):
        @pl.when((step + 1 < n_steps) & ((step + 1) % 2 == sl))
        def _(sl=sl):
            rows(step + 1, sl, False)

    slot = step % 2
    for sl in range(2):
        @pl.when(slot == sl)
        def _(sl=sl):
            rows(step, sl, True)

    comb = comb_ref[...]
    ffn = comb[:, 0:1] * buf[slot, 0]
    for k in range(1, TOP_K):
        ffn = ffn + comb[:, k:k + 1] * buf[slot, k]
    o_ref[...] = _layer_norm(alpha * x1_ref[...] + g2_ref[...] * ffn, lng_ref[...], lnb_ref[...])


def _combine(pos, y, x1, comb, g2, lng, lnb, alpha, per_row_mods):
    d = x1.shape[-1]
    hbm = pl.BlockSpec(memory_space=pl.ANY)
    if per_row_mods:
        n = x1.shape[0]
        tc = min(COMBINE_TILE, n)
        grid = (n // tc,)
        row = lambda w: pl.BlockSpec((tc, w), lambda i, pos: (i, 0))
        mod = row(d)
        const = lambda shp: pl.BlockSpec(shp, lambda i, pos: (0,) * len(shp))
        step_fn = lambda: pl.program_id(0)
        n_steps = n // tc
        sem = ("arbitrary",)
        out_shape = jax.ShapeDtypeStruct((n, d), f32)
    else:
        b, s, _ = x1.shape
        tc = min(COMBINE_TILE, s)
        grid = (b, s // tc)
        row = lambda w: pl.BlockSpec((None, tc, w), lambda bi, i, pos: (bi, i, 0))
        mod = pl.BlockSpec((None, 1, d), lambda bi, i, pos: (bi, 0, 0))
        const = lambda shp: pl.BlockSpec(shp, lambda bi, i, pos: (0,) * len(shp))
        per_b = s // tc
        step_fn = lambda: pl.program_id(0) * per_b + pl.program_id(1)
        n_steps = b * per_b
        sem = ("arbitrary", "arbitrary")
        out_shape = jax.ShapeDtypeStruct((b, s, d), f32)
    return pl.pallas_call(
        functools.partial(_combine_kernel, alpha=alpha, n_steps=n_steps, step_fn=step_fn),
        grid_spec=pltpu.PrefetchScalarGridSpec(
            num_scalar_prefetch=1, grid=grid,
            in_specs=[hbm, row(d), row(LANES), mod, const((1, d)), const((1, d))],
            out_specs=row(d),
            scratch_shapes=[pltpu.VMEM((2, TOP_K, tc, d), f32), pltpu.SemaphoreType.DMA((2,))]),
        out_shape=out_shape,
        compiler_params=_cparams(sem),
        name="moe_combine_rows" if per_row_mods else "moe_combine_prompt",
    )(pos, y, x1, comb, g2, lng, lnb)


def _rope_tables(pos):
    n = pos.shape[0]
    inv_freq = ROPE_THETA ** (-jnp.arange(ROPE_HALF, dtype=f32) * 2.0 / ROPE_DIM)
    ang = pos[:, None] * inv_freq[None, :]
    cos, sin = jnp.cos(ang), jnp.sin(ang)
    rest = HEAD_DIM - ROPE_DIM
    z8, zr = jnp.zeros((n, ROPE_HALF), f32), jnp.zeros((n, rest), f32)
    c = jnp.concatenate([cos, cos, jnp.ones((n, rest), f32)], axis=1)
    s1 = jnp.concatenate([-sin, z8, zr], axis=1)
    s2 = jnp.concatenate([z8, sin, zr], axis=1)
    reps = LANES // HEAD_DIM
    return tuple(jnp.tile(a, (1, reps)) for a in (c, s1, s2))


def _rope_tables_t(pos):
    inv_freq = ROPE_THETA ** (-jnp.arange(ROPE_HALF, dtype=f32) * 2.0 / ROPE_DIM)
    ang = pos[:, None] * inv_freq[None, :]
    return jnp.cos(ang).T, jnp.sin(ang).T


def _pad_cols(a, width, value=0.0):
    return jnp.pad(a, ((0, 0), (0, width - a.shape[1])), constant_values=value)


def _layer(xp, xs, c_prompt, c_sample, cache_k, cache_v, cache_ki, state_conv, page_table, p, alpha):
    b, s, d = xp.shape
    db, t, _ = xs.shape
    ns = db * t
    assert t >= CONV_K - 1 and ns % SUBLANES == 0 and t <= SUBLANES

    n_c = b + db
    c_all = jnp.pad(jnp.concatenate([c_prompt, c_sample], axis=0), ((0, -n_c % SUBLANES), (0, 0)))
    ada = _ada(c_all, p["w_ada"], p["b_ada"])
    mods = [ada[:, j * d:(j + 1) * d] for j in range(6)]
    mp = [m[:b].reshape(b, 1, d) for m in mods]
    ms = [jnp.repeat(m[b:b + db], t, axis=0) for m in mods]

    sizes = (ATT_WIDTH, KV_WIDTH, KV_WIDTH, IDX_HEADS * IDX_DIM, IDX_DIM, IDX_HEADS, d // 2, d // 2, d // 2, d, d)
    cuts = [int(v) for v in np.cumsum(sizes)[:-1]]
    wq, wk, wv, wqi, wki, wwi, wxin, wbg, wcg, wga, wgb = jnp.split(p["w_in"], cuts, axis=1)
    w_proj = jnp.concatenate([wq, wk, wv, wqi, _pad_cols(wki, LANES), _pad_cols(wwi, LANES), wxin, wbg, wcg],
                             axis=1).astype(bf16)
    wg = jnp.concatenate([wga, wgb], axis=1).astype(bf16)
    post_w = (wg, p["w_branch_a"].astype(bf16), p["w_branch_b"].astype(bf16), p["w_o"].astype(bf16),
              _pad_cols(p["w_router"], LANES).astype(bf16), _pad_cols(p["b_router"][None, :], LANES, NEG_BIG),
              p["ln1_g"][None, :], p["ln1_b"][None, :])
    ln2 = (p["ln2_g"][None, :], p["ln2_b"][None, :])

    positions = jnp.arange(s, dtype=f32)
    w_t = jnp.concatenate([wk, wv, wki], axis=1).T.astype(bf16)
    q, qi, wi, kt, vt, kit, kt_b, vt_b, kit_b, conv_p, cst_p = _inproj_prompt(
        xp, mp[1], mp[0], _rope_tables(positions), _rope_tables_t(positions), w_proj, w_t, p["conv_w"])
    att_p = _attn_prompt(qi, wi, q, kit_b, kt_b, vt_b)

    past = page_table.shape[1] * PAGE_SIZE
    tabs_s = _rope_tables(jnp.tile(past + jnp.arange(t, dtype=f32), db))
    zeros = jnp.zeros((db, t, d // 2), f32)
    p1 = zeros.at[:, 0].set(state_conv[:, 1]).reshape(ns, d // 2)
    p2 = zeros.at[:, 0].set(state_conv[:, 0]).at[:, 1].set(state_conv[:, 1]).reshape(ns, d // 2)
    qs, ks, vs, qis, kis, wis, conv_s, u_s = _inproj_sample(xs.reshape(ns, d), ms[1], ms[0], tabs_s, w_proj,
                                                            p["conv_w"], p1, p2, t)
    pad_t = lambda a, rows: jnp.pad(a.reshape(db, t, -1), ((0, 0), (0, rows - t), (0, 0)))
    head_rows = lambda a: pad_t(a, SUBLANES).reshape(db, SUBLANES, -1, HEAD_DIM).transpose(0, 2, 1, 3).reshape(
        db, -1, HEAD_DIM)
    new_cols = lambda a: jnp.pad(a.reshape(db, t, -1).transpose(0, 2, 1), ((0, 0), (0, 0), (0, LANES - t)))
    n_pool = cache_k.shape[0]
    feature_major = lambda c: c.transpose(0, 2, 3, 1).reshape(n_pool, KV_WIDTH, PAGE_SIZE)
    bias_s = _select_sample(page_table, head_rows(qis), pad_t(wis, SUBLANES), new_cols(kis),
                            cache_ki.transpose(0, 2, 1), t)
    att_s = _attn_sample(page_table, head_rows(qs), bias_s, new_cols(ks), new_cols(vs),
                         feature_major(cache_k), feature_major(cache_v))
    att_s = att_s.reshape(db, N_HEADS, SUBLANES, HEAD_DIM)[:, :, :t].transpose(0, 2, 1, 3).reshape(ns, ATT_WIDTH)

    cnt0 = jnp.zeros((1, LANES), f32)
    x1p, h2p, tope_p, comb_p, rank_p, cnt1 = _post(xp, att_p, conv_p, (mp[1], mp[0], mp[2], mp[4], mp[3]),
                                                   post_w, cnt0, alpha, False)
    x1s, h2s, tope_s, comb_s, rank_s, cnt2 = _post(xs.reshape(ns, d), att_s, conv_s, (ms[1], ms[0], ms[2], ms[4], ms[3]),
                                                   post_w, cnt1, alpha, True)

    n_p = b * s
    counts = cnt2[0, :N_EXPERTS].astype(i32)
    padded = (counts + MOE_ROWS - 1) // MOE_ROWS * MOE_ROWS
    pend = jnp.cumsum(padded).astype(i32)
    pstart = pend - padded
    tope = jnp.concatenate([tope_p.reshape(n_p, LANES)[:, :TOP_K], tope_s[:, :TOP_K]], axis=0)
    rank = jnp.concatenate([rank_p.reshape(n_p, LANES)[:, :TOP_K], rank_s[:, :TOP_K]], axis=0)
    pos = (pstart[tope] + rank).astype(i32)
    n_blocks = ((n_p + ns) * TOP_K + N_EXPERTS * (MOE_ROWS - 1)) // MOE_ROWS
    n_used = pend[-1:] // MOE_ROWS
    blk = jnp.minimum(jnp.arange(n_blocks, dtype=i32), n_used[0] - 1)
    block_e = jnp.minimum(jnp.sum(pend[None, :] <= (blk * MOE_ROWS)[:, None], axis=1), N_EXPERTS - 1).astype(i32)
    pos_p, pos_s = pos[:n_p].reshape(-1), pos[n_p:].reshape(-1)

    rows = _dispatch(pos.reshape(-1), pend, h2p.reshape(n_p, d), h2s, n_blocks)
    y = _experts(block_e, n_used, rows, p["w_gate_up"], p["b_gate_up"], p["w_down"], p["b_down"])
    yp = _combine(pos_p, y, x1p, comb_p, mp[5], *ln2, alpha, False)
    ys = _combine(pos_s, y, x1s, comb_s, ms[5], *ln2, alpha, True).reshape(db, t, d)

    token_major = lambda a: a.reshape(b, N_KV_HEADS, HEAD_DIM, s).transpose(0, 3, 1, 2)
    new_p = (token_major(kt), token_major(vt), kit.transpose(0, 2, 1), cst_p)
    new_s = (ks.reshape(db, t, N_KV_HEADS, HEAD_DIM), vs.reshape(db, t, N_KV_HEADS, HEAD_DIM),
             kis.reshape(db, t, IDX_DIM), u_s.reshape(db, t, d // 2)[:, t - (CONV_K - 1):])
    return yp, ys, new_p, new_s


def kernel(x_prompt, x_sample, cache_k, cache_v, cache_idx_k, state_conv, page_table, c_prompt, c_sample,
           w_ada, b_ada, w_in, conv_w, w_branch_a, w_branch_b, w_o, ln1_g, ln1_b, ln2_g, ln2_b,
           w_router, b_router, w_gate_up, b_gate_up, w_down, b_down):
    depth = w_ada.shape[0]
    alpha = float((2 * depth) ** 0.25)
    xp, xs = x_prompt, x_sample
    acc_p, acc_s = [], []
    for l in range(depth):
        p = {"w_ada": w_ada[l], "b_ada": b_ada[l], "w_in": w_in[l], "conv_w": conv_w[l],
             "w_branch_a": w_branch_a[l], "w_branch_b": w_branch_b[l], "w_o": w_o[l],
             "ln1_g": ln1_g[l], "ln1_b": ln1_b[l], "ln2_g": ln2_g[l], "ln2_b": ln2_b[l],
             "w_router": w_router[l], "b_router": b_router[l], "w_gate_up": w_gate_up[l],
             "b_gate_up": b_gate_up[l], "w_down": w_down[l], "b_down": b_down[l]}
        xp, xs, new_p, new_s = _layer(xp, xs, c_prompt, c_sample, cache_k[l], cache_v[l], cache_idx_k[l],
                                      state_conv[l], page_table, p, alpha)
        acc_p.append(new_p)
        acc_s.append(new_s)
    stack = lambda acc, j: jnp.stack([a[j] for a in acc])
    return (xp, xs, stack(acc_p, 0), stack(acc_p, 1), stack(acc_p, 2), stack(acc_p, 3),
            stack(acc_s, 0), stack(acc_s, 1), stack(acc_s, 2), stack(acc_s, 3))
```

```python
import functools

import jax
import jax.numpy as jnp
import numpy as np
from jax import lax
from jax.experimental import pallas as pl
from jax.experimental.pallas import tpu as pltpu

f32 = jnp.float32
bf16 = jnp.bfloat16
i32 = jnp.int32

N_HEADS = 8
HEAD_DIM = 64
N_KV_HEADS = 4
KV_GROUPS = N_HEADS // N_KV_HEADS
ATT_WIDTH = N_HEADS * HEAD_DIM
KV_WIDTH = N_KV_HEADS * HEAD_DIM
ROPE_DIM = HEAD_DIM // 4
ROPE_HALF = ROPE_DIM // 2
ROPE_THETA = 500000.0
IDX_HEADS = 8
IDX_DIM = 64
IDX_W_SCALE = (IDX_HEADS * IDX_DIM) ** -0.5
TOPK_MAX = 256
PAGE_SIZE = 128
CONV_K = 3
N_EXPERTS = 32
TOP_K = 4
SWIGLU_LIMIT = 7.0
SWIGLU_ALPHA = 1.702
LN_EPS = 1e-5

LANES = 128
SUBLANES = 8
VMEM_LIMIT = 56 * 1024 * 1024

ROW_TILE = 256
POST_TILE = 512
MOE_ROWS = 256
COMBINE_TILE = 128
Q_BLOCK = 256
KEY_CLASS = 512
QK_SCALE = HEAD_DIM ** -0.5 * float(np.log2(np.e))
SCORE_CHUNK = 512
SEQ_GROUP = 8
HALF_ROWS = 2 * SUBLANES
BISECT_UNROLL = 4
i16 = jnp.int16
INT_MIN = int(np.iinfo(np.int32).min)
NEG_BIG = -1e30


def _cparams(sem):
    return pltpu.CompilerParams(dimension_semantics=sem, vmem_limit_bytes=VMEM_LIMIT)


def _mm(a, b):
    return jnp.dot(a.astype(bf16), b.astype(bf16), preferred_element_type=f32)


def _mm_nt(a, b):
    return lax.dot_general(a.astype(bf16), b.astype(bf16), (((1,), (1,)), ((), ())), preferred_element_type=f32)


def _layer_norm(y, g, b):
    mu = jnp.mean(y, axis=-1, keepdims=True)
    d = y - mu
    var = jnp.mean(d * d, axis=-1, keepdims=True)
    return d * lax.rsqrt(var + LN_EPS) * g + b


def _ada_kernel(c_ref, w_ref, b_ref, o_ref):
    o_ref[...] = _mm(c_ref[...], w_ref[...]) + b_ref[...]


def _ada(c_all, w_ada, b_ada):
    n, d = c_all.shape
    width = w_ada.shape[1]
    return pl.pallas_call(
        _ada_kernel,
        grid=(width // d,),
        in_specs=[pl.BlockSpec((n, d), lambda j: (0, 0)),
                  pl.BlockSpec((d, d), lambda j: (0, j)),
                  pl.BlockSpec((1, d), lambda j: (0, j))],
        out_specs=pl.BlockSpec((n, d), lambda j: (0, j)),
        out_shape=jax.ShapeDtypeStruct((n, width), f32),
        compiler_params=_cparams(("arbitrary",)),
        name="ada",
    )(c_all, w_ada, b_ada.reshape(1, width))


_SEG = {}
_off = 0
for _name, _w in (("q", ATT_WIDTH), ("k", KV_WIDTH), ("v", KV_WIDTH), ("qi", IDX_HEADS * IDX_DIM),
                  ("ki", LANES), ("wi", LANES), ("xin", 512), ("bg", 512), ("cg", 512)):
    _SEG[_name] = (_off, _off + _w)
    _off += _w
PROJ_WIDTH = _off


def _rope(t, c, s1, s2):
    w = t.shape[1]
    reps = w // LANES
    if reps > 1:
        c, s1, s2 = (jnp.tile(a, (1, reps)) for a in (c, s1, s2))
    return t * c + pltpu.roll(t, w - ROPE_HALF, 1) * s1 + pltpu.roll(t, ROPE_HALF, 1) * s2


def _project(x_ref, sc_ref, sh_ref, cos_ref, s1_ref, s2_ref, w_ref):
    h = x_ref[...] * (1.0 + sc_ref[...]) + sh_ref[...]
    hb = h.astype(bf16)
    c, s1, s2 = cos_ref[...], s1_ref[...], s2_ref[...]

    def seg(name):
        a, b = _SEG[name]
        return jnp.dot(hb, w_ref[:, a:b], preferred_element_type=f32)

    q = _rope(seg("q"), c, s1, s2)
    k = _rope(seg("k"), c, s1, s2)
    v = seg("v")
    qi = _rope(seg("qi"), c, s1, s2)
    ki = _rope(seg("ki"), c, s1, s2)[:, :IDX_DIM]
    wi = seg("wi")
    u = seg("cg") * seg("xin")
    return q, k, v, qi, ki, wi, u, seg("bg")


def _conv_out(u, bg, cw_ref, sh1, sh2):
    y = cw_ref[0:1, :] * sh2 + cw_ref[1:2, :] * sh1
    y = y + cw_ref[2:3, :] * u
    return bg * y


def _rope_rows(t, cos_t, sin_t, n_heads):
    parts = []
    for g in range(n_heads):
        base = g * HEAD_DIM
        r0, r1 = t[base:base + ROPE_HALF], t[base + ROPE_HALF:base + ROPE_DIM]
        parts += [r0 * cos_t - r1 * sin_t, r1 * cos_t + r0 * sin_t, t[base + ROPE_DIM:base + HEAD_DIM]]
    return jnp.concatenate(parts, axis=0)


def _inproj_prompt_kernel(x_ref, sc_ref, sh_ref, cos_ref, s1_ref, s2_ref, cost_ref, sint_ref, w_ref, wt_ref, cw_ref,
                          q_ref, qi_ref, wi_ref, kt_ref, vt_ref, kit_ref, ktb_ref, vtb_ref, kitb_ref,
                          conv_ref, cst_ref, carry_ref):
    @pl.when(pl.program_id(1) == 0)
    def _():
        carry_ref[...] = jnp.zeros_like(carry_ref)

    hb = (x_ref[...] * (1.0 + sc_ref[...]) + sh_ref[...]).astype(bf16)
    c, s1, s2 = cos_ref[...], s1_ref[...], s2_ref[...]

    def seg(name):
        a, b = _SEG[name]
        return jnp.dot(hb, w_ref[:, a:b], preferred_element_type=f32)

    q = _rope(seg("q"), c, s1, s2) * QK_SCALE
    qi = _rope(seg("qi"), c, s1, s2)
    for h in range(N_HEADS):
        q_ref[h] = q[:, h * HEAD_DIM:(h + 1) * HEAD_DIM].astype(bf16)
    for h in range(IDX_HEADS):
        qi_ref[h] = qi[:, h * IDX_DIM:(h + 1) * IDX_DIM].astype(bf16)
    wi_ref[...] = seg("wi") * IDX_W_SCALE

    t = lax.dot_general(wt_ref[...], hb, (((1,), (1,)), ((), ())), preferred_element_type=f32)
    cos_t, sin_t = cost_ref[...], sint_ref[...]
    kt = _rope_rows(t[0:KV_WIDTH], cos_t, sin_t, N_KV_HEADS)
    vt = t[KV_WIDTH:2 * KV_WIDTH]
    kit = _rope_rows(t[2 * KV_WIDTH:2 * KV_WIDTH + IDX_DIM], cos_t, sin_t, 1)
    kt_ref[...], vt_ref[...], kit_ref[...] = kt, vt, kit
    ktb_ref[...], vtb_ref[...], kitb_ref[...] = kt.astype(bf16), vt.astype(bf16), kit.astype(bf16)

    u = seg("cg") * seg("xin")
    bg = seg("bg")
    tm = u.shape[0]
    rows = lax.broadcasted_iota(i32, u.shape, 0)
    c0, c1 = carry_ref[0:1, :], carry_ref[1:2, :]
    sh1 = jnp.where(rows == 0, c1, pltpu.roll(u, 1, 0))
    sh2 = jnp.where(rows == 0, c0, jnp.where(rows == 1, c1, pltpu.roll(u, 2, 0)))
    conv_ref[...] = _conv_out(u, bg, cw_ref, sh1, sh2)
    tail = u[tm - 2:tm, :]
    carry_ref[0:2, :] = tail
    cst_ref[...] = tail


def _inproj_sample_kernel(x_ref, sc_ref, sh_ref, cos_ref, s1_ref, s2_ref, w_ref, cw_ref, p1_ref, p2_ref,
                          q_ref, k_ref, v_ref, qi_ref, ki_ref, wi_ref, conv_ref, u_ref, *, dec_seq):
    q, k, v, qi, ki, wi, u, bg = _project(x_ref, sc_ref, sh_ref, cos_ref, s1_ref, s2_ref, w_ref)
    q_ref[...], k_ref[...], v_ref[...], qi_ref[...], ki_ref[...], wi_ref[...] = q, k, v, qi, ki, wi
    t = lax.broadcasted_iota(i32, u.shape, 0) % dec_seq
    sh1 = jnp.where(t == 0, p1_ref[...], pltpu.roll(u, 1, 0))
    sh2 = jnp.where(t < 2, p2_ref[...], pltpu.roll(u, 2, 0))
    conv_ref[...] = _conv_out(u, bg, cw_ref, sh1, sh2)
    u_ref[...] = u


def _inproj_prompt(x, sc1, sh1, tabs, tabs_t, w_proj, w_t, conv_w):
    b, s, d = x.shape
    tm = min(ROW_TILE, s)
    row = lambda w: pl.BlockSpec((None, tm, w), lambda bi, i: (bi, i, 0))
    heads = lambda n: pl.BlockSpec((None, n, tm, HEAD_DIM), lambda bi, i: (bi, 0, i, 0))
    col = lambda r: pl.BlockSpec((None, r, tm), lambda bi, i: (bi, 0, i))
    per_b = pl.BlockSpec((None, 1, d), lambda bi, i: (bi, 0, 0))
    tab = pl.BlockSpec((tm, LANES), lambda bi, i: (i, 0))
    tab_t = pl.BlockSpec((ROPE_HALF, tm), lambda bi, i: (0, i))
    const = lambda a: pl.BlockSpec(a.shape, lambda bi, i: (0,) * a.ndim)
    feat = (KV_WIDTH, KV_WIDTH, IDX_DIM)
    return pl.pallas_call(
        _inproj_prompt_kernel,
        grid=(b, s // tm),
        in_specs=[row(d), per_b, per_b, tab, tab, tab, tab_t, tab_t, const(w_proj), const(w_t), const(conv_w)],
        out_specs=[heads(N_HEADS), heads(IDX_HEADS), row(LANES)] + [col(r) for r in feat] + [col(r) for r in feat]
        + [row(512), pl.BlockSpec((None, CONV_K - 1, 512), lambda bi, i: (bi, 0, 0))],
        out_shape=[jax.ShapeDtypeStruct((b, N_HEADS, s, HEAD_DIM), bf16),
                   jax.ShapeDtypeStruct((b, IDX_HEADS, s, IDX_DIM), bf16),
                   jax.ShapeDtypeStruct((b, s, LANES), f32)]
        + [jax.ShapeDtypeStruct((b, r, s), f32) for r in feat]
        + [jax.ShapeDtypeStruct((b, r, s), bf16) for r in feat]
        + [jax.ShapeDtypeStruct((b, s, 512), f32), jax.ShapeDtypeStruct((b, CONV_K - 1, 512), f32)],
        scratch_shapes=[pltpu.VMEM((SUBLANES, 512), f32)],
        compiler_params=_cparams(("arbitrary", "arbitrary")),
        name="inproj_prompt",
    )(x, sc1, sh1, *tabs, *tabs_t, w_proj, w_t, conv_w)


def _inproj_sample(x, sc1, sh1, tabs, w_proj, conv_w, p1, p2, dec_seq):
    n, d = x.shape
    widths = (ATT_WIDTH, KV_WIDTH, KV_WIDTH, IDX_HEADS * IDX_DIM, IDX_DIM, LANES, 512, 512)
    full = lambda r, w: pl.BlockSpec((r, w), lambda i: (0, 0))
    return pl.pallas_call(
        functools.partial(_inproj_sample_kernel, dec_seq=dec_seq),
        grid=(1,),
        in_specs=[full(n, d), full(n, d), full(n, d), full(n, LANES), full(n, LANES), full(n, LANES),
                  full(d, PROJ_WIDTH), full(CONV_K, 512), full(n, 512), full(n, 512)],
        out_specs=[full(n, w) for w in widths],
        out_shape=[jax.ShapeDtypeStruct((n, w), f32) for w in widths],
        compiler_params=_cparams(("arbitrary",)),
        name="inproj_sample",
    )(x, sc1, sh1, *tabs, w_proj, conv_w, p1, p2)


def _sort_key(score):
    bits = pltpu.bitcast(jnp.where(score == 0.0, 0.0, score), i32)
    return bits ^ ((bits >> 31) & jnp.int32(0x7FFFFFFF))


def _bisect16(ref, n_cols, need):
    n_groups = ref.shape[0] // HALF_ROWS
    one, zero = jnp.ones((), i16), jnp.zeros((), i16)

    def refine(bit, ts):
        out = []
        for g in range(n_groups):
            rows = slice(g * HALF_ROWS, (g + 1) * HALF_ROWS)
            cand = ts[g] + jnp.left_shift(jnp.int32(1), bit)
            cand16 = cand.astype(i16)
            acc = jnp.zeros((HALF_ROWS, LANES), i16)
            for c in range(n_cols // LANES):
                acc = acc + jnp.where(ref[rows, c * LANES:(c + 1) * LANES] >= cand16, one, zero)
            cnt = jnp.sum(acc.astype(f32), axis=1, keepdims=True)
            out.append(jnp.where(cnt >= need[rows], cand, ts[g]))
        return tuple(out)

    def outer(i, ts):
        for j in range(BISECT_UNROLL):
            ts = refine(15 - (i * BISECT_UNROLL + j), ts)
        return ts

    ts = lax.fori_loop(0, 16 // BISECT_UNROLL, outer,
                       tuple(jnp.full((HALF_ROWS, 1), -32768, i32) for _ in range(n_groups)))
    return jnp.concatenate(ts, axis=0)


def _select_topk(key_ref, hi_ref, lo_ref, n_cols, topk):
    n_rows = key_ref.shape[0]
    kf = float(topk)

    def count(pred):
        return jnp.sum(jnp.where(pred, 1.0, 0.0), axis=1, keepdims=True)

    for c0 in range(0, n_cols, SCORE_CHUNK):
        cols = slice(c0, min(c0 + SCORE_CHUNK, n_cols))
        k = key_ref[:, cols]
        hi_ref[:, cols] = (k >> 16).astype(i16)
        lo_ref[:, cols] = ((k & 0xFFFF) - 32768).astype(i16)
    t_hi = _bisect16(hi_ref, n_cols, jnp.full((n_rows, 1), kf, f32))
    t_hi16 = t_hi.astype(i16)
    acc = jnp.zeros((n_rows, LANES), i16)
    for c in range(n_cols // LANES):
        cols = slice(c * LANES, (c + 1) * LANES)
        h = hi_ref[:, cols]
        acc = acc + jnp.where(h > t_hi16, jnp.ones((), i16), jnp.zeros((), i16))
        lo_ref[:, cols] = jnp.where(h == t_hi16, lo_ref[:, cols], jnp.full((), -32768, i16))
    above = jnp.sum(acc.astype(f32), axis=1, keepdims=True)
    t_lo = _bisect16(lo_ref, n_cols, kf - above)
    t = jnp.left_shift(t_hi, 16) + (t_lo + 32768)
    keys = key_ref[:, 0:n_cols]
    n_ge = count(keys >= t)
    col = lax.broadcasted_iota(i32, (n_rows, n_cols), 1)
    n_bits = int(n_cols).bit_length()

    def tie_search(_):
        need = kf - count(key_ref[:, 0:n_cols] > t)

        def tstep(i, j):
            cand = j + jnp.left_shift(jnp.int32(1), n_bits - 1 - i)
            below = count((key_ref[:, 0:n_cols] == t) & (col < cand))
            return jnp.where(below < need, cand, j)

        return lax.fori_loop(0, n_bits, tstep, jnp.zeros((n_rows, 1), i32))

    jmax = lax.cond(jnp.max(n_ge) > kf, tie_search, lambda _: jnp.full((n_rows, 1), n_cols, i32), 0)
    return (keys > INT_MIN) & ((keys > t) | ((keys == t) & (col <= jmax)))


def _masked_attention(q, kt, vt, bias):
    logits = jnp.dot(q, kt, preferred_element_type=f32) + bias
    m = jnp.max(logits, axis=1, keepdims=True)
    p = jnp.exp2(logits - m)
    den = jnp.sum(p, axis=1, keepdims=True)
    o = lax.dot_general(p.astype(bf16), vt, (((1,), (1,)), ((), ())), preferred_element_type=f32)
    return o / den


def _index_scores(qi_rows, wi, kit, n_heads_rows):
    dots = jnp.dot(qi_rows, kit, preferred_element_type=f32)
    r = n_heads_rows
    score = wi[:, 0:1] * jnp.maximum(dots[0:r], 0.0)
    for h in range(1, IDX_HEADS):
        score = score + wi[:, h:h + 1] * jnp.maximum(dots[h * r:(h + 1) * r], 0.0)
    return score


def _attn_prompt_body(qi_ref, wi_ref, q_ref, kit_ref, kt_ref, vt_ref, o_ref, key_ref, hi_ref, lo_ref, s0, n_keys, topk):
    qi_rows = qi_ref[...].reshape(IDX_HEADS * Q_BLOCK, IDX_DIM)
    wi = wi_ref[...]
    chunk = min(SCORE_CHUNK, n_keys)
    for c0 in range(0, n_keys, chunk):
        score = _index_scores(qi_rows, wi, kit_ref[:, c0:c0 + chunk], Q_BLOCK)
        col = c0 + lax.broadcasted_iota(i32, (Q_BLOCK, chunk), 1)
        qpos = s0 + lax.broadcasted_iota(i32, (Q_BLOCK, chunk), 0)
        key_ref[:, c0:c0 + chunk] = jnp.where(col <= qpos, _sort_key(score), INT_MIN)
    bias = jnp.where(_select_topk(key_ref, hi_ref, lo_ref, n_keys, topk), 0.0, -jnp.inf)
    for h in range(N_HEADS):
        g = h // KV_GROUPS
        o = _masked_attention(q_ref[h], kt_ref[g * HEAD_DIM:(g + 1) * HEAD_DIM, 0:n_keys],
                              vt_ref[g * HEAD_DIM:(g + 1) * HEAD_DIM, 0:n_keys], bias)
        o_ref[:, h * HEAD_DIM:(h + 1) * HEAD_DIM] = o


def _attn_prompt_kernel(qi_ref, wi_ref, q_ref, kit_ref, kt_ref, vt_ref, o_ref, key_ref, hi_ref, lo_ref,
                        *, key_class, topk):
    s0 = pl.program_id(1) * Q_BLOCK
    n_classes = kit_ref.shape[1] // key_class
    cls = (s0 + Q_BLOCK - 1) // key_class
    for c in range(n_classes):
        @pl.when(cls == c)
        def _(c=c):
            _attn_prompt_body(qi_ref, wi_ref, q_ref, kit_ref, kt_ref, vt_ref, o_ref, key_ref, hi_ref, lo_ref,
                              s0, (c + 1) * key_class, topk)


def _attn_prompt(qi, wi, q, kit, kt, vt):
    b, _, s, _ = q.shape
    key_class = min(KEY_CLASS, s)
    topk = min(TOPK_MAX, s // 4)
    blk = lambda w: pl.BlockSpec((None, Q_BLOCK, w), lambda bi, i: (bi, i, 0))
    heads = lambda n: pl.BlockSpec((None, n, Q_BLOCK, HEAD_DIM), lambda bi, i: (bi, 0, i, 0))
    whole = lambda r: pl.BlockSpec((None, r, s), lambda bi, i: (bi, 0, 0))
    return pl.pallas_call(
        functools.partial(_attn_prompt_kernel, key_class=key_class, topk=topk),
        grid=(b, s // Q_BLOCK),
        in_specs=[heads(IDX_HEADS), blk(LANES), heads(N_HEADS), whole(IDX_DIM), whole(KV_WIDTH), whole(KV_WIDTH)],
        out_specs=blk(ATT_WIDTH),
        out_shape=jax.ShapeDtypeStruct((b, s, ATT_WIDTH), f32),
        scratch_shapes=[pltpu.VMEM((Q_BLOCK, s), i32), pltpu.VMEM((Q_BLOCK, s), i16), pltpu.VMEM((Q_BLOCK, s), i16)],
        compiler_params=_cparams(("arbitrary", "arbitrary")),
        name="attn_prompt",
    )(qi, wi, q, kit, kt, vt)


def _select_sample_kernel(pt_ref, qi_ref, wi_ref, kin_ref, cki_ref, bias_ref, kit_buf, key_ref, hi_ref, lo_ref, sem,
                          *, n_pages, dec_seq, topk, group):
    step = pl.program_id(0)
    past = n_pages * PAGE_SIZE
    n_keys = past + LANES

    def copy(j):
        s, p = j // n_pages, j % n_pages
        page = pt_ref[(step * group + s) * n_pages + p]
        cols = pl.ds(pl.multiple_of(p * PAGE_SIZE, PAGE_SIZE), PAGE_SIZE)
        return pltpu.make_async_copy(cki_ref.at[page], kit_buf.at[s, :, cols], sem)

    lax.fori_loop(0, group * n_pages, lambda j, c: (copy(j).start(), c)[1], 0)
    kit_buf[:, :, past:n_keys] = kin_ref[...]
    lax.fori_loop(0, group * n_pages, lambda j, c: (copy(j).wait(), c)[1], 0)

    col = lax.broadcasted_iota(i32, (SUBLANES, n_keys), 1)
    tok = lax.broadcasted_iota(i32, (SUBLANES, n_keys), 0)
    allowed = (col < past) | ((col - past <= tok) & (col - past < dec_seq))
    for s in range(group):
        score = _index_scores(qi_ref[s].astype(bf16), wi_ref[s] * IDX_W_SCALE, kit_buf[s].astype(bf16), SUBLANES)
        key_ref[s * SUBLANES:(s + 1) * SUBLANES, :] = jnp.where(allowed, _sort_key(score), INT_MIN)
    bias = jnp.where(_select_topk(key_ref, hi_ref, lo_ref, n_keys, topk), 0.0, -jnp.inf)
    for s in range(group):
        bias_ref[s] = bias[s * SUBLANES:(s + 1) * SUBLANES, :]


def _select_sample(page_table, qi_rows, wi8, kit_new, cache_kit, dec_seq):
    db, n_pages = page_table.shape
    n_keys = n_pages * PAGE_SIZE + LANES
    topk = min(TOPK_MAX, (n_pages * PAGE_SIZE + dec_seq) // 4)
    group = min(SEQ_GROUP, db)
    per = lambda r, w: pl.BlockSpec((group, r, w), lambda i, pt: (i, 0, 0))
    return pl.pallas_call(
        functools.partial(_select_sample_kernel, n_pages=n_pages, dec_seq=dec_seq, topk=topk, group=group),
        grid_spec=pltpu.PrefetchScalarGridSpec(
            num_scalar_prefetch=1, grid=(db // group,),
            in_specs=[per(IDX_HEADS * SUBLANES, IDX_DIM), per(SUBLANES, LANES), per(IDX_DIM, LANES),
                      pl.BlockSpec(memory_space=pl.ANY)],
            out_specs=per(SUBLANES, n_keys),
            scratch_shapes=[pltpu.VMEM((group, IDX_DIM, n_keys), f32), pltpu.VMEM((group * SUBLANES, n_keys), i32),
                            pltpu.VMEM((group * SUBLANES, n_keys), i16), pltpu.VMEM((group * SUBLANES, n_keys), i16),
                            pltpu.SemaphoreType.DMA(())]),
        out_shape=jax.ShapeDtypeStruct((db, SUBLANES, n_keys), f32),
        compiler_params=_cparams(("arbitrary",)),
        name="select_sample",
    )(page_table.reshape(-1), qi_rows, wi8, kit_new, cache_kit)


def _attn_sample_kernel(pt_ref, q_ref, bias_ref, ktn_ref, vtn_ref, ckt_ref, cvt_ref, o_ref, kt_buf, vt_buf, sem,
                        *, n_pages, n_seqs):
    b = pl.program_id(0)
    past = n_pages * PAGE_SIZE
    n_keys = past + LANES

    def pages(seq, slot, wait):
        def body(p, c):
            page = pt_ref[seq * n_pages + p]
            cols = pl.ds(pl.multiple_of(p * PAGE_SIZE, PAGE_SIZE), PAGE_SIZE)
            for src, dst, j in ((ckt_ref, kt_buf, 0), (cvt_ref, vt_buf, 1)):
                cp = pltpu.make_async_copy(src.at[page], dst.at[slot, :, cols], sem.at[j, slot])
                if wait:
                    cp.wait()
                else:
                    cp.start()
            return c

        lax.fori_loop(0, n_pages, body, 0)

    @pl.when(b == 0)
    def _():
        pages(0, 0, False)

    @pl.when(b + 1 < n_seqs)
    def _():
        pages(b + 1, (b + 1) % 2, False)

    slot = b % 2
    kt_buf[slot, :, past:n_keys] = ktn_ref[...]
    vt_buf[slot, :, past:n_keys] = vtn_ref[...]
    pages(b, slot, True)

    bias = bias_ref[...]
    bias2 = jnp.concatenate([bias] * KV_GROUPS, axis=0)
    rows = KV_GROUPS * SUBLANES
    for g in range(N_KV_HEADS):
        kt = kt_buf[slot, g * HEAD_DIM:(g + 1) * HEAD_DIM, :].astype(bf16)
        vt = vt_buf[slot, g * HEAD_DIM:(g + 1) * HEAD_DIM, :].astype(bf16)
        q = (q_ref[g * rows:(g + 1) * rows, :] * QK_SCALE).astype(bf16)
        o_ref[g * rows:(g + 1) * rows, :] = _masked_attention(q, kt, vt, bias2)


def _attn_sample(page_table, q_rows, bias, kt_new, vt_new, cache_kt, cache_vt):
    db, n_pages = page_table.shape
    n_keys = n_pages * PAGE_SIZE + LANES
    per = lambda r, w: pl.BlockSpec((None, r, w), lambda bi, pt: (bi, 0, 0))
    hbm = pl.BlockSpec(memory_space=pl.ANY)
    return pl.pallas_call(
        functools.partial(_attn_sample_kernel, n_pages=n_pages, n_seqs=db),
        grid_spec=pltpu.PrefetchScalarGridSpec(
            num_scalar_prefetch=1, grid=(db,),
            in_specs=[per(N_HEADS * SUBLANES, HEAD_DIM), per(SUBLANES, n_keys), per(KV_WIDTH, LANES),
                      per(KV_WIDTH, LANES), hbm, hbm],
            out_specs=per(N_HEADS * SUBLANES, HEAD_DIM),
            scratch_shapes=[pltpu.VMEM((2, KV_WIDTH, n_keys), f32), pltpu.VMEM((2, KV_WIDTH, n_keys), f32),
                            pltpu.SemaphoreType.DMA((2, 2))]),
        out_shape=jax.ShapeDtypeStruct((db, N_HEADS * SUBLANES, HEAD_DIM), f32),
        compiler_params=_cparams(("arbitrary",)),
        name="attn_sample",
    )(page_table.reshape(-1), q_rows, bias, kt_new, vt_new, cache_kt, cache_vt)


def _post_body(x_ref, att_ref, conv_ref, sc1_ref, sh1_ref, g1_ref, sc2_ref, sh2_ref,
               wg_ref, wa_ref, wb_ref, wo_ref, wr_ref, br_ref, lng_ref, lnb_ref, cnt_in_ref,
               x1_ref, h2_ref, tope_ref, comb_ref, rank_ref, cnt_ref, *, alpha):
    del cnt_in_ref
    n = x_ref.shape[0]
    sub = min(ROW_TILE, n)
    cnt = cnt_ref[...]
    for r0 in range(0, n, sub):
        cnt = _post_rows(slice(r0, r0 + sub), cnt, x_ref, att_ref, conv_ref, sc1_ref, sh1_ref, g1_ref, sc2_ref,
                         sh2_ref, wg_ref, wa_ref, wb_ref, wo_ref, wr_ref, br_ref, lng_ref, lnb_ref,
                         x1_ref, h2_ref, tope_ref, comb_ref, rank_ref, alpha)
    cnt_ref[...] = cnt


def _post_rows(rows, cnt, x_ref, att_ref, conv_ref, sc1_ref, sh1_ref, g1_ref, sc2_ref, sh2_ref,
               wg_ref, wa_ref, wb_ref, wo_ref, wr_ref, br_ref, lng_ref, lnb_ref,
               x1_ref, h2_ref, tope_ref, comb_ref, rank_ref, alpha):
    mod = lambda ref: ref[...] if ref.shape[0] == 1 else ref[rows, :]
    d = x_ref.shape[-1]
    x = x_ref[rows, :]
    hb = (x * (1.0 + mod(sc1_ref)) + mod(sh1_ref)).astype(bf16)
    ga = jnp.dot(hb, wg_ref[:, 0:d], preferred_element_type=f32)
    gb = jnp.dot(hb, wg_ref[:, d:2 * d], preferred_element_type=f32)
    a = _mm(att_ref[rows, :], wa_ref[...])
    c = _mm(conv_ref[rows, :], wb_ref[...])
    mix = _mm(jax.nn.sigmoid(ga) * a + jax.nn.sigmoid(gb) * c, wo_ref[...])
    x1 = _layer_norm(alpha * x + mod(g1_ref) * mix, lng_ref[...], lnb_ref[...])
    x1_ref[rows, :] = x1
    h2 = x1 * (1.0 + mod(sc2_ref)) + mod(sh2_ref)
    h2_ref[rows, :] = h2
    logits = _mm(h2, wr_ref[...]) + br_ref[...]

    tm = logits.shape[0]
    lane = lax.broadcasted_iota(i32, (tm, LANES), 1)
    tope = jnp.zeros((tm, LANES), i32)
    topv = jnp.full((tm, LANES), -jnp.inf, f32)
    hot = jnp.zeros((tm, LANES), f32)
    picks = []
    work = logits
    for k in range(TOP_K):
        m = jnp.max(work, axis=1, keepdims=True)
        idx = jnp.min(jnp.where(work == m, lane, LANES), axis=1, keepdims=True)
        hit = lane == idx
        tope = jnp.where(lane == k, idx, tope)
        topv = jnp.where(lane == k, m, topv)
        hot = hot + jnp.where(hit, 1.0, 0.0)
        work = jnp.where(hit, -jnp.inf, work)
        picks.append(idx)
    e = jnp.where(lane < TOP_K, jnp.exp(topv - jnp.max(topv, axis=1, keepdims=True)), 0.0)
    comb_ref[rows, :] = e / jnp.sum(e, axis=1, keepdims=True)
    tope_ref[rows, :] = tope

    r_i = lax.broadcasted_iota(i32, (tm, tm), 0)
    c_i = lax.broadcasted_iota(i32, (tm, tm), 1)
    lower = jnp.where(c_i < r_i, 1.0, 0.0)
    before = _mm(lower, hot) + cnt
    rank = jnp.zeros((tm, LANES), f32)
    for k in range(TOP_K):
        rk = jnp.sum(jnp.where(lane == picks[k], before, 0.0), axis=1, keepdims=True)
        rank = jnp.where(lane == k, rk, rank)
    rank_ref[rows, :] = rank.astype(i32)
    return cnt + jnp.sum(hot, axis=0, keepdims=True)


def _post(x, att, conv, mods, weights, cnt_in, alpha, per_row_mods):
    wg, wa, wb, wo, wr, br, lng, lnb = weights
    d = x.shape[-1]
    if per_row_mods:
        n = x.shape[0]
        grid = (1,)
        row = lambda w: pl.BlockSpec((n, w), lambda i: (0, 0))
        mod = row(d)
        const = lambda shp: pl.BlockSpec(shp, lambda i: (0,) * len(shp))
        lead = (n,)
        first = lambda: pl.program_id(0) == 0
        sem = ("arbitrary",)
    else:
        b, s, _ = x.shape
        tm = min(POST_TILE, s)
        grid = (b, s // tm)
        row = lambda w: pl.BlockSpec((None, tm, w), lambda bi, i: (bi, i, 0))
        mod = pl.BlockSpec((None, 1, d), lambda bi, i: (bi, 0, 0))
        const = lambda shp: pl.BlockSpec(shp, lambda bi, i: (0,) * len(shp))
        lead = (b, s)
        first = lambda: (pl.program_id(0) == 0) & (pl.program_id(1) == 0)
        sem = ("arbitrary", "arbitrary")

    def body(*refs):
        cnt_in_ref, cnt_ref = refs[16], refs[22]

        @pl.when(first())
        def _():
            cnt_ref[...] = cnt_in_ref[...]

        _post_body(*refs, alpha=alpha)

    outs = [jax.ShapeDtypeStruct(lead + (d,), f32), jax.ShapeDtypeStruct(lead + (d,), f32),
            jax.ShapeDtypeStruct(lead + (LANES,), i32), jax.ShapeDtypeStruct(lead + (LANES,), f32),
            jax.ShapeDtypeStruct(lead + (LANES,), i32), jax.ShapeDtypeStruct((1, LANES), f32)]
    return pl.pallas_call(
        body,
        grid=grid,
        in_specs=[row(d), row(ATT_WIDTH), row(512), mod, mod, mod, mod, mod,
                  const(wg.shape), const(wa.shape), const(wb.shape), const(wo.shape), const(wr.shape),
                  const(br.shape), const(lng.shape), const(lnb.shape), const((1, LANES))],
        out_specs=[row(d), row(d), row(LANES), row(LANES), row(LANES), const((1, LANES))],
        out_shape=outs,
        compiler_params=_cparams(sem),
        name="post_rows" if per_row_mods else "post_prompt",
    )(x, att, conv, *mods, wg, wa, wb, wo, wr, br, lng, lnb, cnt_in)


def _dispatch_kernel(pos_ref, pend_ref, hp_ref, hs_ref, xs_ref, zero_ref, sem, zsem, *, n_prompt_steps, n_blocks):
    tm = hp_ref.shape[0]
    step = pl.program_id(0)

    @pl.when(step == 0)
    def _():
        zero_ref[...] = jnp.zeros_like(zero_ref)

        def zero_block(row0, wait):
            dst = xs_ref.at[pl.ds(pl.multiple_of(row0, MOE_ROWS), MOE_ROWS), :]
            cp = pltpu.make_async_copy(zero_ref, dst, zsem)
            if wait:
                cp.wait()
            else:
                cp.start()

        def fill(wait):
            def last_block(e, c):
                end = pend_ref[e]
                begin = jnp.where(e == 0, 0, pend_ref[jnp.maximum(e - 1, 0)])

                @pl.when(end > begin)
                def _():
                    zero_block(end - MOE_ROWS, wait)

                return c

            lax.fori_loop(0, N_EXPERTS, last_block, 0)
            n_used = pend_ref[N_EXPERTS - 1] // MOE_ROWS
            lax.fori_loop(n_used, n_blocks, lambda blk, c: (zero_block(blk * MOE_ROWS, wait), c)[1], 0)

        fill(False)
        fill(True)

    def scatter(src_ref, base):
        def issue(r, c):
            for k in range(TOP_K):
                p = pos_ref[(base + r) * TOP_K + k]
                pltpu.make_async_copy(src_ref.at[pl.ds(r, 1), :], xs_ref.at[pl.ds(p, 1), :], sem).start()
            return c

        def drain(r, c):
            for k in range(TOP_K):
                pltpu.make_async_copy(src_ref.at[pl.ds(0, 1), :], xs_ref.at[pl.ds(0, 1), :], sem).wait()
            return c

        lax.fori_loop(0, src_ref.shape[0], issue, 0)
        lax.fori_loop(0, src_ref.shape[0], drain, 0)

    @pl.when(step < n_prompt_steps)
    def _():
        scatter(hp_ref, step * tm)

    @pl.when(step == n_prompt_steps)
    def _():
        scatter(hs_ref, n_prompt_steps * tm)


def _dispatch(pos, pend, h_prompt, h_sample, n_blocks):
    n, d = h_prompt.shape
    ns = h_sample.shape[0]
    tm = min(ROW_TILE, n)
    steps = n // tm
    return pl.pallas_call(
        functools.partial(_dispatch_kernel, n_prompt_steps=steps, n_blocks=n_blocks),
        grid_spec=pltpu.PrefetchScalarGridSpec(
            num_scalar_prefetch=2, grid=(steps + 1,),
            in_specs=[pl.BlockSpec((tm, d), lambda i, pos, pend: (jnp.minimum(i, steps - 1), 0)),
                      pl.BlockSpec((ns, d), lambda i, pos, pend: (0, 0))],
            out_specs=pl.BlockSpec(memory_space=pl.ANY),
            scratch_shapes=[pltpu.VMEM((MOE_ROWS, d), f32), pltpu.SemaphoreType.DMA(()), pltpu.SemaphoreType.DMA(())]),
        out_shape=jax.ShapeDtypeStruct((n_blocks * MOE_ROWS, d), f32),
        compiler_params=_cparams(("arbitrary",)),
        name="moe_dispatch",
    )(pos, pend, h_prompt, h_sample)


def _expert_kernel(be_ref, nu_ref, xs_ref, wgu_ref, bgu_ref, wd_ref, bd_ref, y_ref, wgu_bf, wd_bf):
    b = pl.program_id(0)

    @pl.when(b < nu_ref[0])
    def _():
        @pl.when((b == 0) | (be_ref[b] != be_ref[jnp.maximum(b - 1, 0)]))
        def _():
            wgu_bf[...] = wgu_ref[...].astype(bf16)
            wd_bf[...] = wd_ref[...].astype(bf16)

        ff = wd_bf.shape[0]
        gu = jnp.dot(xs_ref[...].astype(bf16), wgu_bf[...], preferred_element_type=f32) + bgu_ref[...]
        gate = jnp.minimum(gu[:, 0:ff], SWIGLU_LIMIT)
        up = jnp.clip(gu[:, ff:2 * ff], -SWIGLU_LIMIT, SWIGLU_LIMIT)
        act = gate * jax.nn.sigmoid(SWIGLU_ALPHA * gate) * (up + 1.0)
        y_ref[...] = jnp.dot(act.astype(bf16), wd_bf[...], preferred_element_type=f32) + bd_ref[...]

    @pl.when(b >= nu_ref[0])
    def _():
        y_ref[...] = jnp.zeros_like(y_ref)


def _experts(block_e, n_used, xs, w_gate_up, b_gate_up, w_down, b_down):
    n_rows, d = xs.shape
    n_exp, _, ff2 = w_gate_up.shape
    ff = ff2 // 2
    rows = lambda b, be, nu: (b, 0)
    per_e = lambda b, be, nu: (be[b], 0, 0)
    return pl.pallas_call(
        _expert_kernel,
        grid_spec=pltpu.PrefetchScalarGridSpec(
            num_scalar_prefetch=2, grid=(n_rows // MOE_ROWS,),
            in_specs=[pl.BlockSpec((MOE_ROWS, d), rows),
                      pl.BlockSpec((None, d, ff2), per_e), pl.BlockSpec((None, 1, ff2), per_e),
                      pl.BlockSpec((None, ff, d), per_e), pl.BlockSpec((None, 1, d), per_e)],
            out_specs=pl.BlockSpec((MOE_ROWS, d), rows),
            scratch_shapes=[pltpu.VMEM((d, ff2), bf16), pltpu.VMEM((ff, d), bf16)]),
        out_shape=jax.ShapeDtypeStruct((n_rows, d), f32),
        compiler_params=_cparams(("arbitrary",)),
        name="moe_experts",
    )(block_e, n_used, xs, w_gate_up, b_gate_up.reshape(n_exp, 1, ff2), w_down, b_down.reshape(n_exp, 1, d))


def _combine_kernel(pos_ref, y_ref, x1_ref, comb_ref, g2_ref, lng_ref, lnb_ref, o_ref, buf, sem, *, alpha, n_steps, step_fn):
    tc = x1_ref.shape[0]
    step = step_fn()

    def rows(s, slot, wait):
        base = s * tc

        def body(r, c):
            for k in range(TOP_K):
                p = 0 if wait else pos_ref[(base + r) * TOP_K + k]
                cp = pltpu.make_async_copy(y_ref.at[pl.ds(p, 1), :], buf.at[slot, k, pl.ds(0 if wait else r, 1), :],
                                           sem.at[slot])
                if wait:
                    cp.wait()
                else:
                    cp.start()
            return c

        lax.fori_loop(0, tc, body, 0)

    @pl.when(step == 0)
    def _():
        rows(0, 0, False)

    for sl in range(2):
        @pl.when((step + 1 < n_steps) & ((step + 1) % 2 == sl))
        def _(sl=sl):
            rows(step + 1, sl, False)

    slot = step % 2
    for sl in range(2):
        @pl.when(slot == sl)
        def _(sl=sl):
            rows(step, sl, True)

    comb = comb_ref[...]
    ffn = comb[:, 0:1] * buf[slot, 0]
    for k in range(1, TOP_K):
        ffn = ffn + comb[:, k:k + 1] * buf[slot, k]
    o_ref[...] = _layer_norm(alpha * x1_ref[...] + g2_ref[...] * ffn, lng_ref[...], lnb_ref[...])


def _combine(pos, y, x1, comb, g2, lng, lnb, alpha, per_row_mods):
    d = x1.shape[-1]
    hbm = pl.BlockSpec(memory_space=pl.ANY)
    if per_row_mods:
        n = x1.shape[0]
        tc = min(COMBINE_TILE, n)
        grid = (n // tc,)
        row = lambda w: pl.BlockSpec((tc, w), lambda i, pos: (i, 0))
        mod = row(d)
        const = lambda shp: pl.BlockSpec(shp, lambda i, pos: (0,) * len(shp))
        step_fn = lambda: pl.program_id(0)
        n_steps = n // tc
        sem = ("arbitrary",)
        out_shape = jax.ShapeDtypeStruct((n, d), f32)
    else:
        b, s, _ = x1.shape
        tc = min(COMBINE_TILE, s)
        grid = (b, s // tc)
        row = lambda w: pl.BlockSpec((None, tc, w), lambda bi, i, pos: (bi, i, 0))
        mod = pl.BlockSpec((None, 1, d), lambda bi, i, pos: (bi, 0, 0))
        const = lambda shp: pl.BlockSpec(shp, lambda bi, i, pos: (0,) * len(shp))
        per_b = s // tc
        step_fn = lambda: pl.program_id(0) * per_b + pl.program_id(1)
        n_steps = b * per_b
        sem = ("arbitrary", "arbitrary")
        out_shape = jax.ShapeDtypeStruct((b, s, d), f32)
    return pl.pallas_call(
        functools.partial(_combine_kernel, alpha=alpha, n_steps=n_steps, step_fn=step_fn),
        grid_spec=pltpu.PrefetchScalarGridSpec(
            num_scalar_prefetch=1, grid=grid,
            in_specs=[hbm, row(d), row(LANES), mod, const((1, d)), const((1, d))],
            out_specs=row(d),
            scratch_shapes=[pltpu.VMEM((2, TOP_K, tc, d), f32), pltpu.SemaphoreType.DMA((2,))]),
        out_shape=out_shape,
        compiler_params=_cparams(sem),
        name="moe_combine_rows" if per_row_mods else "moe_combine_prompt",
    )(pos, y, x1, comb, g2, lng, lnb)


def _rope_tables(pos):
    n = pos.shape[0]
    inv_freq = ROPE_THETA ** (-jnp.arange(ROPE_HALF, dtype=f32) * 2.0 / ROPE_DIM)
    ang = pos[:, None] * inv_freq[None, :]
    cos, sin = jnp.cos(ang), jnp.sin(ang)
    rest = HEAD_DIM - ROPE_DIM
    z8, zr = jnp.zeros((n, ROPE_HALF), f32), jnp.zeros((n, rest), f32)
    c = jnp.concatenate([cos, cos, jnp.ones((n, rest), f32)], axis=1)
    s1 = jnp.concatenate([-sin, z8, zr], axis=1)
    s2 = jnp.concatenate([z8, sin, zr], axis=1)
    reps = LANES // HEAD_DIM
    return tuple(jnp.tile(a, (1, reps)) for a in (c, s1, s2))


def _rope_tables_t(pos):
    inv_freq = ROPE_THETA ** (-jnp.arange(ROPE_HALF, dtype=f32) * 2.0 / ROPE_DIM)
    ang = pos[:, None] * inv_freq[None, :]
    return jnp.cos(ang).T, jnp.sin(ang).T


def _pad_cols(a, width, value=0.0):
    return jnp.pad(a, ((0, 0), (0, width - a.shape[1])), constant_values=value)


def _layer(xp, xs, c_prompt, c_sample, cache_k, cache_v, cache_ki, state_conv, page_table, p, alpha):
    b, s, d = xp.shape
    db, t, _ = xs.shape
    ns = db * t
    assert t >= CONV_K - 1 and ns % SUBLANES == 0 and t <= SUBLANES

    n_c = b + db
    c_all = jnp.pad(jnp.concatenate([c_prompt, c_sample], axis=0), ((0, -n_c % SUBLANES), (0, 0)))
    ada = _ada(c_all, p["w_ada"], p["b_ada"])
    mods = [ada[:, j * d:(j + 1) * d] for j in range(6)]
    mp = [m[:b].reshape(b, 1, d) for m in mods]
    ms = [jnp.repeat(m[b:b + db], t, axis=0) for m in mods]

    sizes = (ATT_WIDTH, KV_WIDTH, KV_WIDTH, IDX_HEADS * IDX_DIM, IDX_DIM, IDX_HEADS, d // 2, d // 2, d // 2, d, d)
    cuts = [int(v) for v in np.cumsum(sizes)[:-1]]
    wq, wk, wv, wqi, wki, wwi, wxin, wbg, wcg, wga, wgb = jnp.split(p["w_in"], cuts, axis=1)
    w_proj = jnp.concatenate([wq, wk, wv, wqi, _pad_cols(wki, LANES), _pad_cols(wwi, LANES), wxin, wbg, wcg],
                             axis=1).astype(bf16)
    wg = jnp.concatenate([wga, wgb], axis=1).astype(bf16)
    post_w = (wg, p["w_branch_a"].astype(bf16), p["w_branch_b"].astype(bf16), p["w_o"].astype(bf16),
              _pad_cols(p["w_router"], LANES).astype(bf16), _pad_cols(p["b_router"][None, :], LANES, NEG_BIG),
              p["ln1_g"][None, :], p["ln1_b"][None, :])
    ln2 = (p["ln2_g"][None, :], p["ln2_b"][None, :])

    positions = jnp.arange(s, dtype=f32)
    w_t = jnp.concatenate([wk, wv, wki], axis=1).T.astype(bf16)
    q, qi, wi, kt, vt, kit, kt_b, vt_b, kit_b, conv_p, cst_p = _inproj_prompt(
        xp, mp[1], mp[0], _rope_tables(positions), _rope_tables_t(positions), w_proj, w_t, p["conv_w"])
    att_p = _attn_prompt(qi, wi, q, kit_b, kt_b, vt_b)

    past = page_table.shape[1] * PAGE_SIZE
    tabs_s = _rope_tables(jnp.tile(past + jnp.arange(t, dtype=f32), db))
    zeros = jnp.zeros((db, t, d // 2), f32)
    p1 = zeros.at[:, 0].set(state_conv[:, 1]).reshape(ns, d // 2)
    p2 = zeros.at[:, 0].set(state_conv[:, 0]).at[:, 1].set(state_conv[:, 1]).reshape(ns, d // 2)
    qs, ks, vs, qis, kis, wis, conv_s, u_s = _inproj_sample(xs.reshape(ns, d), ms[1], ms[0], tabs_s, w_proj,
                                                            p["conv_w"], p1, p2, t)
    pad_t = lambda a, rows: jnp.pad(a.reshape(db, t, -1), ((0, 0), (0, rows - t), (0, 0)))
    head_rows = lambda a: pad_t(a, SUBLANES).reshape(db, SUBLANES, -1, HEAD_DIM).transpose(0, 2, 1, 3).reshape(
        db, -1, HEAD_DIM)
    new_cols = lambda a: jnp.pad(a.reshape(db, t, -1).transpose(0, 2, 1), ((0, 0), (0, 0), (0, LANES - t)))
    n_pool = cache_k.shape[0]
    feature_major = lambda c: c.transpose(0, 2, 3, 1).reshape(n_pool, KV_WIDTH, PAGE_SIZE)
    bias_s = _select_sample(page_table, head_rows(qis), pad_t(wis, SUBLANES), new_cols(kis),
                            cache_ki.transpose(0, 2, 1), t)
    att_s = _attn_sample(page_table, head_rows(qs), bias_s, new_cols(ks), new_cols(vs),
                         feature_major(cache_k), feature_major(cache_v))
    att_s = att_s.reshape(db, N_HEADS, SUBLANES, HEAD_DIM)[:, :, :t].transpose(0, 2, 1, 3).reshape(ns, ATT_WIDTH)

    cnt0 = jnp.zeros((1, LANES), f32)
    x1p, h2p, tope_p, comb_p, rank_p, cnt1 = _post(xp, att_p, conv_p, (mp[1], mp[0], mp[2], mp[4], mp[3]),
                                                   post_w, cnt0, alpha, False)
    x1s, h2s, tope_s, comb_s, rank_s, cnt2 = _post(xs.reshape(ns, d), att_s, conv_s, (ms[1], ms[0], ms[2], ms[4], ms[3]),
                                                   post_w, cnt1, alpha, True)

    n_p = b * s
    counts = cnt2[0, :N_EXPERTS].astype(i32)
    padded = (counts + MOE_ROWS - 1) // MOE_ROWS * MOE_ROWS
    pend = jnp.cumsum(padded).astype(i32)
    pstart = pend - padded
    tope = jnp.concatenate([tope_p.reshape(n_p, LANES)[:, :TOP_K], tope_s[:, :TOP_K]], axis=0)
    rank = jnp.concatenate([rank_p.reshape(n_p, LANES)[:, :TOP_K], rank_s[:, :TOP_K]], axis=0)
    pos = (pstart[tope] + rank).astype(i32)
    n_blocks = ((n_p + ns) * TOP_K + N_EXPERTS * (MOE_ROWS - 1)) // MOE_ROWS
    n_used = pend[-1:] // MOE_ROWS
    blk = jnp.minimum(jnp.arange(n_blocks, dtype=i32), n_used[0] - 1)
    block_e = jnp.minimum(jnp.sum(pend[None, :] <= (blk * MOE_ROWS)[:, None], axis=1), N_EXPERTS - 1).astype(i32)
    pos_p, pos_s = pos[:n_p].reshape(-1), pos[n_p:].reshape(-1)

    rows = _dispatch(pos.reshape(-1), pend, h2p.reshape(n_p, d), h2s, n_blocks)
    y = _experts(block_e, n_used, rows, p["w_gate_up"], p["b_gate_up"], p["w_down"], p["b_down"])
    yp = _combine(pos_p, y, x1p, comb_p, mp[5], *ln2, alpha, False)
    ys = _combine(pos_s, y, x1s, comb_s, ms[5], *ln2, alpha, True).reshape(db, t, d)

    token_major = lambda a: a.reshape(b, N_KV_HEADS, HEAD_DIM, s).transpose(0, 3, 1, 2)
    new_p = (token_major(kt), token_major(vt), kit.transpose(0, 2, 1), cst_p)
    new_s = (ks.reshape(db, t, N_KV_HEADS, HEAD_DIM), vs.reshape(db, t, N_KV_HEADS, HEAD_DIM),
             kis.reshape(db, t, IDX_DIM), u_s.reshape(db, t, d // 2)[:, t - (CONV_K - 1):])
    return yp, ys, new_p, new_s


def kernel(x_prompt, x_sample, cache_k, cache_v, cache_idx_k, state_conv, page_table, c_prompt, c_sample,
           w_ada, b_ada, w_in, conv_w, w_branch_a, w_branch_b, w_o, ln1_g, ln1_b, ln2_g, ln2_b,
           w_router, b_router, w_gate_up, b_gate_up, w_down, b_down):
    depth = w_ada.shape[0]
    alpha = float((2 * depth) ** 0.25)
    xp, xs = x_prompt, x_sample
    acc_p, acc_s = [], []
    for l in range(depth):
        p = {"w_ada": w_ada[l], "b_ada": b_ada[l], "w_in": w_in[l], "conv_w": conv_w[l],
             "w_branch_a": w_branch_a[l], "w_branch_b": w_branch_b[l], "w_o": w_o[l],
             "ln1_g": ln1_g[l], "ln1_b": ln1_b[l], "ln2_g": ln2_g[l], "ln2_b": ln2_b[l],
             "w_router": w_router[l], "b_router": b_router[l], "w_gate_up": w_gate_up[l],
             "b_gate_up": b_gate_up[l], "w_down": w_down[l], "b_down": b_down[l]}
        xp, xs, new_p, new_s = _layer(xp, xs, c_prompt, c_sample, cache_k[l], cache_v[l], cache_idx_k[l],
                                      state_conv[l], page_table, p, alpha)
        acc_p.append(new_p)
        acc_s.append(new_s)
    stack = lambda acc, j: jnp.stack([a[j] for a in acc])
    return (xp, xs, stack(acc_p, 0), stack(acc_p, 1), stack(acc_p, 2), stack(acc_p, 3),
            stack(acc_s, 0), stack(acc_s, 1), stack(acc_s, 2), stack(acc_s, 3))
```

```python
import functools

import jax
import jax.numpy as jnp
import numpy as np
from jax import lax
from jax.experimental import pallas as pl
from jax.experimental.pallas import tpu as pltpu

f32 = jnp.float32
bf16 = jnp.bfloat16
i32 = jnp.int32

N_HEADS = 8
HEAD_DIM = 64
N_KV_HEADS = 4
KV_GROUPS = N_HEADS // N_KV_HEADS
ATT_WIDTH = N_HEADS * HEAD_DIM
KV_WIDTH = N_KV_HEADS * HEAD_DIM
ROPE_DIM = HEAD_DIM // 4
ROPE_HALF = ROPE_DIM // 2
ROPE_THETA = 500000.0
IDX_HEADS = 8
IDX_DIM = 64
IDX_W_SCALE = (IDX_HEADS * IDX_DIM) ** -0.5
TOPK_MAX = 256
PAGE_SIZE = 128
CONV_K = 3
N_EXPERTS = 32
TOP_K = 4
SWIGLU_LIMIT = 7.0
SWIGLU_ALPHA = 1.702
LN_EPS = 1e-5

LANES = 128
SUBLANES = 8
VMEM_LIMIT = 56 * 1024 * 1024

ROW_TILE = 256
POST_TILE = 512
MOE_ROWS = 256
COMBINE_TILE = 128
Q_BLOCK = 128
KEY_CLASS = 1024
QK_SCALE = HEAD_DIM ** -0.5 * float(np.log2(np.e))
SCORE_CHUNK = 512
SEQ_GROUP = 8
HALF_ROWS = 2 * SUBLANES
BISECT_UNROLL = 4
WIDE_SEARCH_COLS = 2048
i16 = jnp.int16
INT_MIN = int(np.iinfo(np.int32).min)
NEG_BIG = -1e30


def _cparams(sem):
    return pltpu.CompilerParams(dimension_semantics=sem, vmem_limit_bytes=VMEM_LIMIT)


def _mm(a, b):
    return jnp.dot(a.astype(bf16), b.astype(bf16), preferred_element_type=f32)


def _mm_nt(a, b):
    return lax.dot_general(a.astype(bf16), b.astype(bf16), (((1,), (1,)), ((), ())), preferred_element_type=f32)


def _layer_norm(y, g, b):
    mu = jnp.mean(y, axis=-1, keepdims=True)
    d = y - mu
    var = jnp.mean(d * d, axis=-1, keepdims=True)
    return d * lax.rsqrt(var + LN_EPS) * g + b


def _ada_kernel(c_ref, w_ref, b_ref, o_ref):
    o_ref[...] = _mm(c_ref[...], w_ref[...]) + b_ref[...]


def _ada(c_all, w_ada, b_ada):
    n, d = c_all.shape
    width = w_ada.shape[1]
    return pl.pallas_call(
        _ada_kernel,
        grid=(width // d,),
        in_specs=[pl.BlockSpec((n, d), lambda j: (0, 0)),
                  pl.BlockSpec((d, d), lambda j: (0, j)),
                  pl.BlockSpec((1, d), lambda j: (0, j))],
        out_specs=pl.BlockSpec((n, d), lambda j: (0, j)),
        out_shape=jax.ShapeDtypeStruct((n, width), f32),
        compiler_params=_cparams(("arbitrary",)),
        name="ada",
    )(c_all, w_ada, b_ada.reshape(1, width))


_SEG = {}
_off = 0
for _name, _w in (("q", ATT_WIDTH), ("k", KV_WIDTH), ("v", KV_WIDTH), ("qi", IDX_HEADS * IDX_DIM),
                  ("ki", LANES), ("wi", LANES), ("xin", 512), ("bg", 512), ("cg", 512)):
    _SEG[_name] = (_off, _off + _w)
    _off += _w
PROJ_WIDTH = _off


def _rope(t, c, s1, s2):
    w = t.shape[1]
    reps = w // LANES
    if reps > 1:
        c, s1, s2 = (jnp.tile(a, (1, reps)) for a in (c, s1, s2))
    return t * c + pltpu.roll(t, w - ROPE_HALF, 1) * s1 + pltpu.roll(t, ROPE_HALF, 1) * s2


def _project(x_ref, sc_ref, sh_ref, cos_ref, s1_ref, s2_ref, w_ref):
    h = x_ref[...] * (1.0 + sc_ref[...]) + sh_ref[...]
    hb = h.astype(bf16)
    c, s1, s2 = cos_ref[...], s1_ref[...], s2_ref[...]

    def seg(name):
        a, b = _SEG[name]
        return jnp.dot(hb, w_ref[:, a:b], preferred_element_type=f32)

    q = _rope(seg("q"), c, s1, s2)
    k = _rope(seg("k"), c, s1, s2)
    v = seg("v")
    qi = _rope(seg("qi"), c, s1, s2)
    ki = _rope(seg("ki"), c, s1, s2)[:, :IDX_DIM]
    wi = seg("wi")
    u = seg("cg") * seg("xin")
    return q, k, v, qi, ki, wi, u, seg("bg")


def _conv_out(u, bg, cw_ref, sh1, sh2):
    y = cw_ref[0:1, :] * sh2 + cw_ref[1:2, :] * sh1
    y = y + cw_ref[2:3, :] * u
    return bg * y


def _rope_rows(t, cos_t, sin_t, n_heads):
    parts = []
    for g in range(n_heads):
        base = g * HEAD_DIM
        r0, r1 = t[base:base + ROPE_HALF], t[base + ROPE_HALF:base + ROPE_DIM]
        parts += [r0 * cos_t - r1 * sin_t, r1 * cos_t + r0 * sin_t, t[base + ROPE_DIM:base + HEAD_DIM]]
    return jnp.concatenate(parts, axis=0)


def _inproj_prompt_kernel(x_ref, sc_ref, sh_ref, cos_ref, s1_ref, s2_ref, cost_ref, sint_ref, w_ref, wt_ref, cw_ref,
                          q_ref, qi_ref, wi_ref, kt_ref, vt_ref, kit_ref, ktb_ref, vtb_ref, kitb_ref,
                          conv_ref, cst_ref, carry_ref):
    @pl.when(pl.program_id(1) == 0)
    def _():
        carry_ref[...] = jnp.zeros_like(carry_ref)

    hb = (x_ref[...] * (1.0 + sc_ref[...]) + sh_ref[...]).astype(bf16)
    c, s1, s2 = cos_ref[...], s1_ref[...], s2_ref[...]

    def seg(name):
        a, b = _SEG[name]
        return jnp.dot(hb, w_ref[:, a:b], preferred_element_type=f32)

    q = _rope(seg("q"), c, s1, s2) * QK_SCALE
    qi = _rope(seg("qi"), c, s1, s2)
    for h in range(N_HEADS):
        q_ref[h] = q[:, h * HEAD_DIM:(h + 1) * HEAD_DIM].astype(bf16)
    for h in range(IDX_HEADS):
        qi_ref[h] = qi[:, h * IDX_DIM:(h + 1) * IDX_DIM].astype(bf16)
    wi_ref[...] = seg("wi") * IDX_W_SCALE

    t = lax.dot_general(wt_ref[...], hb, (((1,), (1,)), ((), ())), preferred_element_type=f32)
    cos_t, sin_t = cost_ref[...], sint_ref[...]
    kt = _rope_rows(t[0:KV_WIDTH], cos_t, sin_t, N_KV_HEADS)
    vt = t[KV_WIDTH:2 * KV_WIDTH]
    kit = _rope_rows(t[2 * KV_WIDTH:2 * KV_WIDTH + IDX_DIM], cos_t, sin_t, 1)
    kt_ref[...], vt_ref[...], kit_ref[...] = kt, vt, kit
    ktb_ref[...], vtb_ref[...], kitb_ref[...] = kt.astype(bf16), vt.astype(bf16), kit.astype(bf16)

    u = seg("cg") * seg("xin")
    bg = seg("bg")
    tm = u.shape[0]
    rows = lax.broadcasted_iota(i32, u.shape, 0)
    c0, c1 = carry_ref[0:1, :], carry_ref[1:2, :]
    sh1 = jnp.where(rows == 0, c1, pltpu.roll(u, 1, 0))
    sh2 = jnp.where(rows == 0, c0, jnp.where(rows == 1, c1, pltpu.roll(u, 2, 0)))
    conv_ref[...] = _conv_out(u, bg, cw_ref, sh1, sh2)
    tail = u[tm - 2:tm, :]
    carry_ref[0:2, :] = tail
    cst_ref[...] = tail


def _inproj_sample_kernel(x_ref, sc_ref, sh_ref, cos_ref, s1_ref, s2_ref, w_ref, cw_ref, p1_ref, p2_ref,
                          q_ref, k_ref, v_ref, qi_ref, ki_ref, wi_ref, conv_ref, u_ref, *, dec_seq):
    q, k, v, qi, ki, wi, u, bg = _project(x_ref, sc_ref, sh_ref, cos_ref, s1_ref, s2_ref, w_ref)
    q_ref[...], k_ref[...], v_ref[...], qi_ref[...], ki_ref[...], wi_ref[...] = q, k, v, qi, ki, wi
    t = lax.broadcasted_iota(i32, u.shape, 0) % dec_seq
    sh1 = jnp.where(t == 0, p1_ref[...], pltpu.roll(u, 1, 0))
    sh2 = jnp.where(t < 2, p2_ref[...], pltpu.roll(u, 2, 0))
    conv_ref[...] = _conv_out(u, bg, cw_ref, sh1, sh2)
    u_ref[...] = u


def _inproj_prompt(x, sc1, sh1, tabs, tabs_t, w_proj, w_t, conv_w):
    b, s, d = x.shape
    tm = min(ROW_TILE, s)
    row = lambda w: pl.BlockSpec((None, tm, w), lambda bi, i: (bi, i, 0))
    heads = lambda n: pl.BlockSpec((None, n, tm, HEAD_DIM), lambda bi, i: (bi, 0, i, 0))
    col = lambda r: pl.BlockSpec((None, r, tm), lambda bi, i: (bi, 0, i))
    per_b = pl.BlockSpec((None, 1, d), lambda bi, i: (bi, 0, 0))
    tab = pl.BlockSpec((tm, LANES), lambda bi, i: (i, 0))
    tab_t = pl.BlockSpec((ROPE_HALF, tm), lambda bi, i: (0, i))
    const = lambda a: pl.BlockSpec(a.shape, lambda bi, i: (0,) * a.ndim)
    feat = (KV_WIDTH, KV_WIDTH, IDX_DIM)
    return pl.pallas_call(
        _inproj_prompt_kernel,
        grid=(b, s // tm),
        in_specs=[row(d), per_b, per_b, tab, tab, tab, tab_t, tab_t, const(w_proj), const(w_t), const(conv_w)],
        out_specs=[heads(N_HEADS), heads(IDX_HEADS), row(LANES)] + [col(r) for r in feat] + [col(r) for r in feat]
        + [row(512), pl.BlockSpec((None, CONV_K - 1, 512), lambda bi, i: (bi, 0, 0))],
        out_shape=[jax.ShapeDtypeStruct((b, N_HEADS, s, HEAD_DIM), bf16),
                   jax.ShapeDtypeStruct((b, IDX_HEADS, s, IDX_DIM), bf16),
                   jax.ShapeDtypeStruct((b, s, LANES), f32)]
        + [jax.ShapeDtypeStruct((b, r, s), f32) for r in feat]
        + [jax.ShapeDtypeStruct((b, r, s), bf16) for r in feat]
        + [jax.ShapeDtypeStruct((b, s, 512), f32), jax.ShapeDtypeStruct((b, CONV_K - 1, 512), f32)],
        scratch_shapes=[pltpu.VMEM((SUBLANES, 512), f32)],
        compiler_params=_cparams(("arbitrary", "arbitrary")),
        name="inproj_prompt",
    )(x, sc1, sh1, *tabs, *tabs_t, w_proj, w_t, conv_w)


def _inproj_sample(x, sc1, sh1, tabs, w_proj, conv_w, p1, p2, dec_seq):
    n, d = x.shape
    widths = (ATT_WIDTH, KV_WIDTH, KV_WIDTH, IDX_HEADS * IDX_DIM, IDX_DIM, LANES, 512, 512)
    full = lambda r, w: pl.BlockSpec((r, w), lambda i: (0, 0))
    return pl.pallas_call(
        functools.partial(_inproj_sample_kernel, dec_seq=dec_seq),
        grid=(1,),
        in_specs=[full(n, d), full(n, d), full(n, d), full(n, LANES), full(n, LANES), full(n, LANES),
                  full(d, PROJ_WIDTH), full(CONV_K, 512), full(n, 512), full(n, 512)],
        out_specs=[full(n, w) for w in widths],
        out_shape=[jax.ShapeDtypeStruct((n, w), f32) for w in widths],
        compiler_params=_cparams(("arbitrary",)),
        name="inproj_sample",
    )(x, sc1, sh1, *tabs, w_proj, conv_w, p1, p2)


def _sort_key(score):
    bits = pltpu.bitcast(jnp.where(score == 0.0, 0.0, score), i32)
    return bits ^ ((bits >> 31) & jnp.int32(0x7FFFFFFF))


def _bisect16(ref, n_cols, need):
    n_groups = ref.shape[0] // HALF_ROWS
    one, zero = jnp.ones((), i16), jnp.zeros((), i16)
    bits_per_step = 2 if n_cols <= WIDE_SEARCH_COLS else 1

    def refine(bit, ts):
        out = []
        for g in range(n_groups):
            rows = slice(g * HALF_ROWS, (g + 1) * HALF_ROWS)
            low = bit - (bits_per_step - 1)
            cands = [ts[g] + jnp.left_shift(jnp.int32(v), low) for v in range(1, 2 ** bits_per_step)]
            cands16 = [c.astype(i16) for c in cands]
            accs = [jnp.zeros((HALF_ROWS, LANES), i16) for _ in cands]
            for c in range(n_cols // LANES):
                x = ref[rows, c * LANES:(c + 1) * LANES]
                accs = [a + jnp.where(x >= c16, one, zero) for a, c16 in zip(accs, cands16)]
            best = ts[g]
            for a, cand in zip(accs, cands):
                cnt = jnp.sum(a.astype(f32), axis=1, keepdims=True)
                best = jnp.where(cnt >= need[rows], cand, best)
            out.append(best)
        return tuple(out)

    def outer(i, ts):
        for j in range(0, BISECT_UNROLL, bits_per_step):
            ts = refine(15 - (i * BISECT_UNROLL + j), ts)
        return ts

    ts = lax.fori_loop(0, 16 // BISECT_UNROLL, outer,
                       tuple(jnp.full((HALF_ROWS, 1), -32768, i32) for _ in range(n_groups)))
    return jnp.concatenate(ts, axis=0)


def _select_topk(key_ref, hi_ref, lo_ref, n_cols, topk):
    n_rows = key_ref.shape[0]
    kf = float(topk)

    def count(pred):
        return jnp.sum(jnp.where(pred, 1.0, 0.0), axis=1, keepdims=True)

    for c0 in range(0, n_cols, SCORE_CHUNK):
        cols = slice(c0, min(c0 + SCORE_CHUNK, n_cols))
        k = key_ref[:, cols]
        hi_ref[:, cols] = (k >> 16).astype(i16)
        lo_ref[:, cols] = ((k & 0xFFFF) - 32768).astype(i16)
    t_hi = _bisect16(hi_ref, n_cols, jnp.full((n_rows, 1), kf, f32))
    t_hi16 = t_hi.astype(i16)
    acc = jnp.zeros((n_rows, LANES), i16)
    for c in range(n_cols // LANES):
        cols = slice(c * LANES, (c + 1) * LANES)
        h = hi_ref[:, cols]
        acc = acc + jnp.where(h > t_hi16, jnp.ones((), i16), jnp.zeros((), i16))
        lo_ref[:, cols] = jnp.where(h == t_hi16, lo_ref[:, cols], jnp.full((), -32768, i16))
    above = jnp.sum(acc.astype(f32), axis=1, keepdims=True)
    t_lo = _bisect16(lo_ref, n_cols, kf - above)
    t = jnp.left_shift(t_hi, 16) + (t_lo + 32768)
    keys = key_ref[:, 0:n_cols]
    n_ge = count(keys >= t)
    col = lax.broadcasted_iota(i32, (n_rows, n_cols), 1)
    n_bits = int(n_cols).bit_length()

    def tie_search(_):
        need = kf - count(key_ref[:, 0:n_cols] > t)

        def tstep(i, j):
            cand = j + jnp.left_shift(jnp.int32(1), n_bits - 1 - i)
            below = count((key_ref[:, 0:n_cols] == t) & (col < cand))
            return jnp.where(below < need, cand, j)

        return lax.fori_loop(0, n_bits, tstep, jnp.zeros((n_rows, 1), i32))

    jmax = lax.cond(jnp.max(n_ge) > kf, tie_search, lambda _: jnp.full((n_rows, 1), n_cols, i32), 0)
    return (keys > INT_MIN) & ((keys > t) | ((keys == t) & (col <= jmax)))


def _masked_attention(q, kt, vt, bias):
    logits = jnp.dot(q, kt, preferred_element_type=f32) + bias
    m = jnp.max(logits, axis=1, keepdims=True)
    p = jnp.exp2(logits - m)
    den = jnp.sum(p, axis=1, keepdims=True)
    o = lax.dot_general(p.astype(bf16), vt, (((1,), (1,)), ((), ())), preferred_element_type=f32)
    return o / den


def _index_scores(qi_rows, wi, kit, n_heads_rows):
    dots = jnp.dot(qi_rows, kit, preferred_element_type=f32)
    r = n_heads_rows
    score = wi[:, 0:1] * jnp.maximum(dots[0:r], 0.0)
    for h in range(1, IDX_HEADS):
        score = score + wi[:, h:h + 1] * jnp.maximum(dots[h * r:(h + 1) * r], 0.0)
    return score


def _attn_prompt_body(qi_ref, wi_ref, q_ref, kit_ref, kt_ref, vt_ref, o_ref, key_ref, hi_ref, lo_ref,
                      s0, n_keys, topk):
    qi_rows = qi_ref[...].reshape(IDX_HEADS * Q_BLOCK, IDX_DIM)
    wi = wi_ref[...]
    chunk = min(SCORE_CHUNK, n_keys)

    for c0 in range(0, n_keys, chunk):
        score = _index_scores(qi_rows, wi, kit_ref[:, c0:c0 + chunk], Q_BLOCK)
        col = c0 + lax.broadcasted_iota(i32, (Q_BLOCK, chunk), 1)
        qpos = s0 + lax.broadcasted_iota(i32, (Q_BLOCK, chunk), 0)
        key_ref[:, c0:c0 + chunk] = jnp.where(col <= qpos, _sort_key(score), INT_MIN)
    bias = jnp.where(_select_topk(key_ref, hi_ref, lo_ref, n_keys, topk), 0.0, -jnp.inf)
    for h in range(N_HEADS):
        g = h // KV_GROUPS
        o = _masked_attention(q_ref[h], kt_ref[g * HEAD_DIM:(g + 1) * HEAD_DIM, 0:n_keys],
                              vt_ref[g * HEAD_DIM:(g + 1) * HEAD_DIM, 0:n_keys], bias)
        o_ref[:, h * HEAD_DIM:(h + 1) * HEAD_DIM] = o


def _attn_prompt_kernel(qi_ref, wi_ref, q_ref, kit_ref, kt_ref, vt_ref, o_ref, key_ref, hi_ref, lo_ref,
                        *, key_class, topk):
    s0 = pl.program_id(1) * Q_BLOCK
    n_classes = kit_ref.shape[1] // key_class
    cls = (s0 + Q_BLOCK - 1) // key_class
    for c in range(n_classes):
        @pl.when(cls == c)
        def _(c=c):
            _attn_prompt_body(qi_ref, wi_ref, q_ref, kit_ref, kt_ref, vt_ref, o_ref, key_ref, hi_ref, lo_ref,
                              s0, (c + 1) * key_class, topk)


def _attn_prompt(qi, wi, q, kit, kt, vt):
    b, _, s, _ = q.shape
    key_class = min(KEY_CLASS, s)
    topk = min(TOPK_MAX, s // 4)
    blk = lambda w: pl.BlockSpec((None, Q_BLOCK, w), lambda bi, i: (bi, i, 0))
    heads = lambda n: pl.BlockSpec((None, n, Q_BLOCK, HEAD_DIM), lambda bi, i: (bi, 0, i, 0))
    whole = lambda r: pl.BlockSpec((None, r, s), lambda bi, i: (bi, 0, 0))
    return pl.pallas_call(
        functools.partial(_attn_prompt_kernel, key_class=key_class, topk=topk),
        grid=(b, s // Q_BLOCK),
        in_specs=[heads(IDX_HEADS), blk(LANES), heads(N_HEADS), whole(IDX_DIM), whole(KV_WIDTH), whole(KV_WIDTH)],
        out_specs=blk(ATT_WIDTH),
        out_shape=jax.ShapeDtypeStruct((b, s, ATT_WIDTH), f32),
        scratch_shapes=[pltpu.VMEM((Q_BLOCK, s), i32), pltpu.VMEM((Q_BLOCK, s), i16), pltpu.VMEM((Q_BLOCK, s), i16)],
        compiler_params=_cparams(("arbitrary", "arbitrary")),
        name="attn_prompt",
    )(qi, wi, q, kit, kt, vt)


def _select_sample_kernel(pt_ref, qi_ref, wi_ref, kin_ref, cki_ref, bias_ref, kit_buf, key_ref, hi_ref, lo_ref, sem,
                          *, n_pages, dec_seq, topk, group):
    step = pl.program_id(0)
    past = n_pages * PAGE_SIZE
    n_keys = past + LANES

    def copy(j):
        s, p = j // n_pages, j % n_pages
        page = pt_ref[(step * group + s) * n_pages + p]
        cols = pl.ds(pl.multiple_of(p * PAGE_SIZE, PAGE_SIZE), PAGE_SIZE)
        return pltpu.make_async_copy(cki_ref.at[page], kit_buf.at[s, :, cols], sem)

    lax.fori_loop(0, group * n_pages, lambda j, c: (copy(j).start(), c)[1], 0)
    kit_buf[:, :, past:n_keys] = kin_ref[...]
    lax.fori_loop(0, group * n_pages, lambda j, c: (copy(j).wait(), c)[1], 0)

    col = lax.broadcasted_iota(i32, (SUBLANES, n_keys), 1)
    tok = lax.broadcasted_iota(i32, (SUBLANES, n_keys), 0)
    allowed = (col < past) | ((col - past <= tok) & (col - past < dec_seq))
    for s in range(group):
        score = _index_scores(qi_ref[s].astype(bf16), wi_ref[s] * IDX_W_SCALE, kit_buf[s].astype(bf16), SUBLANES)
        key_ref[s * SUBLANES:(s + 1) * SUBLANES, :] = jnp.where(allowed, _sort_key(score), INT_MIN)
    bias = jnp.where(_select_topk(key_ref, hi_ref, lo_ref, n_keys, topk), 0.0, -jnp.inf)
    for s in range(group):
        bias_ref[s] = bias[s * SUBLANES:(s + 1) * SUBLANES, :]


def _select_sample(page_table, qi_rows, wi8, kit_new, cache_kit, dec_seq):
    db, n_pages = page_table.shape
    n_keys = n_pages * PAGE_SIZE + LANES
    topk = min(TOPK_MAX, (n_pages * PAGE_SIZE + dec_seq) // 4)
    group = min(SEQ_GROUP, db)
    per = lambda r, w: pl.BlockSpec((group, r, w), lambda i, pt: (i, 0, 0))
    return pl.pallas_call(
        functools.partial(_select_sample_kernel, n_pages=n_pages, dec_seq=dec_seq, topk=topk, group=group),
        grid_spec=pltpu.PrefetchScalarGridSpec(
            num_scalar_prefetch=1, grid=(db // group,),
            in_specs=[per(IDX_HEADS * SUBLANES, IDX_DIM), per(SUBLANES, LANES), per(IDX_DIM, LANES),
                      pl.BlockSpec(memory_space=pl.ANY)],
            out_specs=per(SUBLANES, n_keys),
            scratch_shapes=[pltpu.VMEM((group, IDX_DIM, n_keys), f32), pltpu.VMEM((group * SUBLANES, n_keys), i32),
                            pltpu.VMEM((group * SUBLANES, n_keys), i16), pltpu.VMEM((group * SUBLANES, n_keys), i16),
                            pltpu.SemaphoreType.DMA(())]),
        out_shape=jax.ShapeDtypeStruct((db, SUBLANES, n_keys), f32),
        compiler_params=_cparams(("arbitrary",)),
        name="select_sample",
    )(page_table.reshape(-1), qi_rows, wi8, kit_new, cache_kit)


def _attn_sample_kernel(pt_ref, q_ref, bias_ref, ktn_ref, vtn_ref, ckt_ref, cvt_ref, o_ref, kt_buf, vt_buf, sem,
                        *, n_pages, n_seqs):
    b = pl.program_id(0)
    past = n_pages * PAGE_SIZE
    n_keys = past + LANES

    def pages(seq, slot, wait):
        def body(p, c):
            page = pt_ref[seq * n_pages + p]
            cols = pl.ds(pl.multiple_of(p * PAGE_SIZE, PAGE_SIZE), PAGE_SIZE)
            for src, dst, j in ((ckt_ref, kt_buf, 0), (cvt_ref, vt_buf, 1)):
                cp = pltpu.make_async_copy(src.at[page], dst.at[slot, :, cols], sem.at[j, slot])
                if wait:
                    cp.wait()
                else:
                    cp.start()
            return c

        lax.fori_loop(0, n_pages, body, 0)

    @pl.when(b == 0)
    def _():
        pages(0, 0, False)

    @pl.when(b + 1 < n_seqs)
    def _():
        pages(b + 1, (b + 1) % 2, False)

    slot = b % 2
    kt_buf[slot, :, past:n_keys] = ktn_ref[...]
    vt_buf[slot, :, past:n_keys] = vtn_ref[...]
    pages(b, slot, True)

    bias = bias_ref[...]
    bias2 = jnp.concatenate([bias] * KV_GROUPS, axis=0)
    rows = KV_GROUPS * SUBLANES
    for g in range(N_KV_HEADS):
        kt = kt_buf[slot, g * HEAD_DIM:(g + 1) * HEAD_DIM, :].astype(bf16)
        vt = vt_buf[slot, g * HEAD_DIM:(g + 1) * HEAD_DIM, :].astype(bf16)
        q = (q_ref[g * rows:(g + 1) * rows, :] * QK_SCALE).astype(bf16)
        o_ref[g * rows:(g + 1) * rows, :] = _masked_attention(q, kt, vt, bias2)


def _attn_sample(page_table, q_rows, bias, kt_new, vt_new, cache_kt, cache_vt):
    db, n_pages = page_table.shape
    n_keys = n_pages * PAGE_SIZE + LANES
    per = lambda r, w: pl.BlockSpec((None, r, w), lambda bi, pt: (bi, 0, 0))
    hbm = pl.BlockSpec(memory_space=pl.ANY)
    return pl.pallas_call(
        functools.partial(_attn_sample_kernel, n_pages=n_pages, n_seqs=db),
        grid_spec=pltpu.PrefetchScalarGridSpec(
            num_scalar_prefetch=1, grid=(db,),
            in_specs=[per(N_HEADS * SUBLANES, HEAD_DIM), per(SUBLANES, n_keys), per(KV_WIDTH, LANES),
                      per(KV_WIDTH, LANES), hbm, hbm],
            out_specs=per(N_HEADS * SUBLANES, HEAD_DIM),
            scratch_shapes=[pltpu.VMEM((2, KV_WIDTH, n_keys), f32), pltpu.VMEM((2, KV_WIDTH, n_keys), f32),
                            pltpu.SemaphoreType.DMA((2, 2))]),
        out_shape=jax.ShapeDtypeStruct((db, N_HEADS * SUBLANES, HEAD_DIM), f32),
        compiler_params=_cparams(("arbitrary",)),
        name="attn_sample",
    )(page_table.reshape(-1), q_rows, bias, kt_new, vt_new, cache_kt, cache_vt)


def _post_body(x_ref, att_ref, conv_ref, sc1_ref, sh1_ref, g1_ref, sc2_ref, sh2_ref,
               wg_ref, wa_ref, wb_ref, wo_ref, wr_ref, br_ref, lng_ref, lnb_ref, cnt_in_ref,
               x1_ref, h2_ref, tope_ref, comb_ref, rank_ref, cnt_ref, *, alpha):
    del cnt_in_ref
    n = x_ref.shape[0]
    sub = min(ROW_TILE, n)
    cnt = cnt_ref[...]
    for r0 in range(0, n, sub):
        cnt = _post_rows(slice(r0, r0 + sub), cnt, x_ref, att_ref, conv_ref, sc1_ref, sh1_ref, g1_ref, sc2_ref,
                         sh2_ref, wg_ref, wa_ref, wb_ref, wo_ref, wr_ref, br_ref, lng_ref, lnb_ref,
                         x1_ref, h2_ref, tope_ref, comb_ref, rank_ref, alpha)
    cnt_ref[...] = cnt


def _post_rows(rows, cnt, x_ref, att_ref, conv_ref, sc1_ref, sh1_ref, g1_ref, sc2_ref, sh2_ref,
               wg_ref, wa_ref, wb_ref, wo_ref, wr_ref, br_ref, lng_ref, lnb_ref,
               x1_ref, h2_ref, tope_ref, comb_ref, rank_ref, alpha):
    mod = lambda ref: ref[...] if ref.shape[0] == 1 else ref[rows, :]
    d = x_ref.shape[-1]
    x = x_ref[rows, :]
    hb = (x * (1.0 + mod(sc1_ref)) + mod(sh1_ref)).astype(bf16)
    ga = jnp.dot(hb, wg_ref[:, 0:d], preferred_element_type=f32)
    gb = jnp.dot(hb, wg_ref[:, d:2 * d], preferred_element_type=f32)
    a = _mm(att_ref[rows, :], wa_ref[...])
    c = _mm(conv_ref[rows, :], wb_ref[...])
    mix = _mm(jax.nn.sigmoid(ga) * a + jax.nn.sigmoid(gb) * c, wo_ref[...])
    x1 = _layer_norm(alpha * x + mod(g1_ref) * mix, lng_ref[...], lnb_ref[...])
    x1_ref[rows, :] = x1
    h2 = x1 * (1.0 + mod(sc2_ref)) + mod(sh2_ref)
    h2_ref[rows, :] = h2
    logits = _mm(h2, wr_ref[...]) + br_ref[...]

    tm = logits.shape[0]
    lane = lax.broadcasted_iota(i32, (tm, LANES), 1)
    tope = jnp.zeros((tm, LANES), i32)
    topv = jnp.full((tm, LANES), -jnp.inf, f32)
    hot = jnp.zeros((tm, LANES), f32)
    picks = []
    work = logits
    for k in range(TOP_K):
        m = jnp.max(work, axis=1, keepdims=True)
        idx = jnp.min(jnp.where(work == m, lane, LANES), axis=1, keepdims=True)
        hit = lane == idx
        tope = jnp.where(lane == k, idx, tope)
        topv = jnp.where(lane == k, m, topv)
        hot = hot + jnp.where(hit, 1.0, 0.0)
        work = jnp.where(hit, -jnp.inf, work)
        picks.append(idx)
    e = jnp.where(lane < TOP_K, jnp.exp(topv - jnp.max(topv, axis=1, keepdims=True)), 0.0)
    comb_ref[rows, :] = e / jnp.sum(e, axis=1, keepdims=True)
    tope_ref[rows, :] = tope

    r_i = lax.broadcasted_iota(i32, (tm, tm), 0)
    c_i = lax.broadcasted_iota(i32, (tm, tm), 1)
    lower = jnp.where(c_i < r_i, 1.0, 0.0)
    before = _mm(lower, hot) + cnt
    rank = jnp.zeros((tm, LANES), f32)
    for k in range(TOP_K):
        rk = jnp.sum(jnp.where(lane == picks[k], before, 0.0), axis=1, keepdims=True)
        rank = jnp.where(lane == k, rk, rank)
    rank_ref[rows, :] = rank.astype(i32)
    return cnt + jnp.sum(hot, axis=0, keepdims=True)


def _post(x, att, conv, mods, weights, cnt_in, alpha, per_row_mods):
    wg, wa, wb, wo, wr, br, lng, lnb = weights
    d = x.shape[-1]
    if per_row_mods:
        n = x.shape[0]
        grid = (1,)
        row = lambda w: pl.BlockSpec((n, w), lambda i: (0, 0))
        mod = row(d)
        const = lambda shp: pl.BlockSpec(shp, lambda i: (0,) * len(shp))
        lead = (n,)
        first = lambda: pl.program_id(0) == 0
        sem = ("arbitrary",)
    else:
        b, s, _ = x.shape
        tm = min(POST_TILE, s)
        grid = (b, s // tm)
        row = lambda w: pl.BlockSpec((None, tm, w), lambda bi, i: (bi, i, 0))
        mod = pl.BlockSpec((None, 1, d), lambda bi, i: (bi, 0, 0))
        const = lambda shp: pl.BlockSpec(shp, lambda bi, i: (0,) * len(shp))
        lead = (b, s)
        first = lambda: (pl.program_id(0) == 0) & (pl.program_id(1) == 0)
        sem = ("arbitrary", "arbitrary")

    def body(*refs):
        cnt_in_ref, cnt_ref = refs[16], refs[22]

        @pl.when(first())
        def _():
            cnt_ref[...] = cnt_in_ref[...]

        _post_body(*refs, alpha=alpha)

    outs = [jax.ShapeDtypeStruct(lead + (d,), f32), jax.ShapeDtypeStruct(lead + (d,), f32),
            jax.ShapeDtypeStruct(lead + (LANES,), i32), jax.ShapeDtypeStruct(lead + (LANES,), f32),
            jax.ShapeDtypeStruct(lead + (LANES,), i32), jax.ShapeDtypeStruct((1, LANES), f32)]
    return pl.pallas_call(
        body,
        grid=grid,
        in_specs=[row(d), row(ATT_WIDTH), row(512), mod, mod, mod, mod, mod,
                  const(wg.shape), const(wa.shape), const(wb.shape), const(wo.shape), const(wr.shape),
                  const(br.shape), const(lng.shape), const(lnb.shape), const((1, LANES))],
        out_specs=[row(d), row(d), row(LANES), row(LANES), row(LANES), const((1, LANES))],
        out_shape=outs,
        compiler_params=_cparams(sem),
        name="post_rows" if per_row_mods else "post_prompt",
    )(x, att, conv, *mods, wg, wa, wb, wo, wr, br, lng, lnb, cnt_in)


def _dispatch_kernel(pos_ref, pend_ref, hp_ref, hs_ref, xs_ref, zero_ref, sem, zsem, *, n_prompt_steps, n_blocks):
    tm = hp_ref.shape[0]
    step = pl.program_id(0)

    @pl.when(step == 0)
    def _():
        zero_ref[...] = jnp.zeros_like(zero_ref)

        def zero_block(row0, wait):
            dst = xs_ref.at[pl.ds(pl.multiple_of(row0, MOE_ROWS), MOE_ROWS), :]
            cp = pltpu.make_async_copy(zero_ref, dst, zsem)
            if wait:
                cp.wait()
            else:
                cp.start()

        def fill(wait):
            def last_block(e, c):
                end = pend_ref[e]
                begin = jnp.where(e == 0, 0, pend_ref[jnp.maximum(e - 1, 0)])

                @pl.when(end > begin)
                def _():
                    zero_block(end - MOE_ROWS, wait)

                return c

            lax.fori_loop(0, N_EXPERTS, last_block, 0)
            n_used = pend_ref[N_EXPERTS - 1] // MOE_ROWS
            lax.fori_loop(n_used, n_blocks, lambda blk, c: (zero_block(blk * MOE_ROWS, wait), c)[1], 0)

        fill(False)
        fill(True)

    def scatter(src_ref, base):
        def issue(r, c):
            for k in range(TOP_K):
                p = pos_ref[(base + r) * TOP_K + k]
                pltpu.make_async_copy(src_ref.at[pl.ds(r, 1), :], xs_ref.at[pl.ds(p, 1), :], sem).start()
            return c

        def drain(r, c):
            for k in range(TOP_K):
                pltpu.make_async_copy(src_ref.at[pl.ds(0, 1), :], xs_ref.at[pl.ds(0, 1), :], sem).wait()
            return c

        lax.fori_loop(0, src_ref.shape[0], issue, 0)
        lax.fori_loop(0, src_ref.shape[0], drain, 0)

    @pl.when(step < n_prompt_steps)
    def _():
        scatter(hp_ref, step * tm)

    @pl.when(step == n_prompt_steps)
    def _():
        scatter(hs_ref, n_prompt_steps * tm)


def _dispatch(pos, pend, h_prompt, h_sample, n_blocks):
    n, d = h_prompt.shape
    ns = h_sample.shape[0]
    tm = min(ROW_TILE, n)
    steps = n // tm
    return pl.pallas_call(
        functools.partial(_dispatch_kernel, n_prompt_steps=steps, n_blocks=n_blocks),
        grid_spec=pltpu.PrefetchScalarGridSpec(
            num_scalar_prefetch=2, grid=(steps + 1,),
            in_specs=[pl.BlockSpec((tm, d), lambda i, pos, pend: (jnp.minimum(i, steps - 1), 0)),
                      pl.BlockSpec((ns, d), lambda i, pos, pend: (0, 0))],
            out_specs=pl.BlockSpec(memory_space=pl.ANY),
            scratch_shapes=[pltpu.VMEM((MOE_ROWS, d), f32), pltpu.SemaphoreType.DMA(()), pltpu.SemaphoreType.DMA(())]),
        out_shape=jax.ShapeDtypeStruct((n_blocks * MOE_ROWS, d), f32),
        compiler_params=_cparams(("arbitrary",)),
        name="moe_dispatch",
    )(pos, pend, h_prompt, h_sample)


def _expert_kernel(be_ref, nu_ref, xs_ref, wgu_ref, bgu_ref, wd_ref, bd_ref, y_ref, wgu_bf, wd_bf):
    b = pl.program_id(0)

    @pl.when(b < nu_ref[0])
    def _():
        @pl.when((b == 0) | (be_ref[b] != be_ref[jnp.maximum(b - 1, 0)]))
        def _():
            wgu_bf[...] = wgu_ref[...].astype(bf16)
            wd_bf[...] = wd_ref[...].astype(bf16)

        ff = wd_bf.shape[0]
        gu = jnp.dot(xs_ref[...].astype(bf16), wgu_bf[...], preferred_element_type=f32) + bgu_ref[...]
        gate = jnp.minimum(gu[:, 0:ff], SWIGLU_LIMIT)
        up = jnp.clip(gu[:, ff:2 * ff], -SWIGLU_LIMIT, SWIGLU_LIMIT)
        act = gate * jax.nn.sigmoid(SWIGLU_ALPHA * gate) * (up + 1.0)
        y_ref[...] = jnp.dot(act.astype(bf16), wd_bf[...], preferred_element_type=f32) + bd_ref[...]

    @pl.when(b >= nu_ref[0])
    def _():
        y_ref[...] = jnp.zeros_like(y_ref)


def _experts(block_e, n_used, xs, w_gate_up, b_gate_up, w_down, b_down):
    n_rows, d = xs.shape
    n_exp, _, ff2 = w_gate_up.shape
    ff = ff2 // 2
    rows = lambda b, be, nu: (b, 0)
    per_e = lambda b, be, nu: (be[b], 0, 0)
    return pl.pallas_call(
        _expert_kernel,
        grid_spec=pltpu.PrefetchScalarGridSpec(
            num_scalar_prefetch=2, grid=(n_rows // MOE_ROWS,),
            in_specs=[pl.BlockSpec((MOE_ROWS, d), rows),
                      pl.BlockSpec((None, d, ff2), per_e), pl.BlockSpec((None, 1, ff2), per_e),
                      pl.BlockSpec((None, ff, d), per_e), pl.BlockSpec((None, 1, d), per_e)],
            out_specs=pl.BlockSpec((MOE_ROWS, d), rows),
            scratch_shapes=[pltpu.VMEM((d, ff2), bf16), pltpu.VMEM((ff, d), bf16)]),
        out_shape=jax.ShapeDtypeStruct((n_rows, d), f32),
        compiler_params=_cparams(("arbitrary",)),
        name="moe_experts",
    )(block_e, n_used, xs, w_gate_up, b_gate_up.reshape(n_exp, 1, ff2), w_down, b_down.reshape(n_exp, 1, d))


def _combine_kernel(pos_ref, y_ref, x1_ref, comb_ref, g2_ref, lng_ref, lnb_ref, o_ref, buf, sem, *, alpha, n_steps, step_fn):
    tc = x1_ref.shape[0]
    step = step_fn()

    def rows(s, slot, wait):
        base = s * tc

        def body(r, c):
            for k in range(TOP_K):
                p = 0 if wait else pos_ref[(base + r) * TOP_K + k]
                cp = pltpu.make_async_copy(y_ref.at[pl.ds(p, 1), :], buf.at[slot, k, pl.ds(0 if wait else r, 1), :],
                                           sem.at[slot])
                if wait:
                    cp.wait()
                else:
                    cp.start()
            return c

        lax.fori_loop(0, tc, body, 0)

    @pl.when(step == 0)
    def _():
        rows(0, 0, False)

    for sl in range(2):
        @pl.when((step + 1 < n_steps) & ((step + 1) % 2 == sl))
        def _(sl=sl):
            rows(step + 1, sl, False)

    slot = step % 2
    for sl in range(2):
        @pl.when(slot == sl)
        def _(sl=sl):
            rows(step, sl, True)

    comb = comb_ref[...]
    ffn = comb[:, 0:1] * buf[slot, 0]
    for k in range(1, TOP_K):
        ffn = ffn + comb[:, k:k + 1] * buf[slot, k]
    o_ref[...] = _layer_norm(alpha * x1_ref[...] + g2_ref[...] * ffn, lng_ref[...], lnb_ref[...])


def _combine(pos, y, x1, comb, g2, lng, lnb, alpha, per_row_mods):
    d = x1.shape[-1]
    hbm = pl.BlockSpec(memory_space=pl.ANY)
    if per_row_mods:
        n = x1.shape[0]
        tc = min(COMBINE_TILE, n)
        grid = (n // tc,)
        row = lambda w: pl.BlockSpec((tc, w), lambda i, pos: (i, 0))
        mod = row(d)
        const = lambda shp: pl.BlockSpec(shp, lambda i, pos: (0,) * len(shp))
        step_fn = lambda: pl.program_id(0)
        n_steps = n // tc
        sem = ("arbitrary",)
        out_shape = jax.ShapeDtypeStruct((n, d), f32)
    else:
        b, s, _ = x1.shape
        tc = min(COMBINE_TILE, s)
        grid = (b, s // tc)
        row = lambda w: pl.BlockSpec((None, tc, w), lambda bi, i, pos: (bi, i, 0))
        mod = pl.BlockSpec((None, 1, d), lambda bi, i, pos: (bi, 0, 0))
        const = lambda shp: pl.BlockSpec(shp, lambda bi, i, pos: (0,) * len(shp))
        per_b = s // tc
        step_fn = lambda: pl.program_id(0) * per_b + pl.program_id(1)
        n_steps = b * per_b
        sem = ("arbitrary", "arbitrary")
        out_shape = jax.ShapeDtypeStruct((b, s, d), f32)
    return pl.pallas_call(
        functools.partial(_combine_kernel, alpha=alpha, n_steps=n_steps, step_fn=step_fn),
        grid_spec=pltpu.PrefetchScalarGridSpec(
            num_scalar_prefetch=1, grid=grid,
            in_specs=[hbm, row(d), row(LANES), mod, const((1, d)), const((1, d))],
            out_specs=row(d),
            scratch_shapes=[pltpu.VMEM((2, TOP_K, tc, d), f32), pltpu.SemaphoreType.DMA((2,))]),
        out_shape=out_shape,
        compiler_params=_cparams(sem),
        name="moe_combine_rows" if per_row_mods else "moe_combine_prompt",
    )(pos, y, x1, comb, g2, lng, lnb)


def _rope_tables(pos):
    n = pos.shape[0]
    inv_freq = ROPE_THETA ** (-jnp.arange(ROPE_HALF, dtype=f32) * 2.0 / ROPE_DIM)
    ang = pos[:, None] * inv_freq[None, :]
    cos, sin = jnp.cos(ang), jnp.sin(ang)
    rest = HEAD_DIM - ROPE_DIM
    z8, zr = jnp.zeros((n, ROPE_HALF), f32), jnp.zeros((n, rest), f32)
    c = jnp.concatenate([cos, cos, jnp.ones((n, rest), f32)], axis=1)
    s1 = jnp.concatenate([-sin, z8, zr], axis=1)
    s2 = jnp.concatenate([z8, sin, zr], axis=1)
    reps = LANES // HEAD_DIM
    return tuple(jnp.tile(a, (1, reps)) for a in (c, s1, s2))


def _rope_tables_t(pos):
    inv_freq = ROPE_THETA ** (-jnp.arange(ROPE_HALF, dtype=f32) * 2.0 / ROPE_DIM)
    ang = pos[:, None] * inv_freq[None, :]
    return jnp.cos(ang).T, jnp.sin(ang).T


def _pad_cols(a, width, value=0.0):
    return jnp.pad(a, ((0, 0), (0, width - a.shape[1])), constant_values=value)


def _layer(xp, xs, c_prompt, c_sample, cache_k, cache_v, cache_ki, state_conv, page_table, p, alpha):
    b, s, d = xp.shape
    db, t, _ = xs.shape
    ns = db * t
    assert t >= CONV_K - 1 and ns % SUBLANES == 0 and t <= SUBLANES

    n_c = b + db
    c_all = jnp.pad(jnp.concatenate([c_prompt, c_sample], axis=0), ((0, -n_c % SUBLANES), (0, 0)))
    ada = _ada(c_all, p["w_ada"], p["b_ada"])
    mods = [ada[:, j * d:(j + 1) * d] for j in range(6)]
    mp = [m[:b].reshape(b, 1, d) for m in mods]
    ms = [jnp.repeat(m[b:b + db], t, axis=0) for m in mods]

    sizes = (ATT_WIDTH, KV_WIDTH, KV_WIDTH, IDX_HEADS * IDX_DIM, IDX_DIM, IDX_HEADS, d // 2, d // 2, d // 2, d, d)
    cuts = [int(v) for v in np.cumsum(sizes)[:-1]]
    wq, wk, wv, wqi, wki, wwi, wxin, wbg, wcg, wga, wgb = jnp.split(p["w_in"], cuts, axis=1)
    w_proj = jnp.concatenate([wq, wk, wv, wqi, _pad_cols(wki, LANES), _pad_cols(wwi, LANES), wxin, wbg, wcg],
                             axis=1).astype(bf16)
    wg = jnp.concatenate([wga, wgb], axis=1).astype(bf16)
    post_w = (wg, p["w_branch_a"].astype(bf16), p["w_branch_b"].astype(bf16), p["w_o"].astype(bf16),
              _pad_cols(p["w_router"], LANES).astype(bf16), _pad_cols(p["b_router"][None, :], LANES, NEG_BIG),
              p["ln1_g"][None, :], p["ln1_b"][None, :])
    ln2 = (p["ln2_g"][None, :], p["ln2_b"][None, :])

    positions = jnp.arange(s, dtype=f32)
    w_t = jnp.concatenate([wk, wv, wki], axis=1).T.astype(bf16)
    q, qi, wi, kt, vt, kit, kt_b, vt_b, kit_b, conv_p, cst_p = _inproj_prompt(
        xp, mp[1], mp[0], _rope_tables(positions), _rope_tables_t(positions), w_proj, w_t, p["conv_w"])
    att_p = _attn_prompt(qi, wi, q, kit_b, kt_b, vt_b)

    past = page_table.shape[1] * PAGE_SIZE
    tabs_s = _rope_tables(jnp.tile(past + jnp.arange(t, dtype=f32), db))
    zeros = jnp.zeros((db, t, d // 2), f32)
    p1 = zeros.at[:, 0].set(state_conv[:, 1]).reshape(ns, d // 2)
    p2 = zeros.at[:, 0].set(state_conv[:, 0]).at[:, 1].set(state_conv[:, 1]).reshape(ns, d // 2)
    qs, ks, vs, qis, kis, wis, conv_s, u_s = _inproj_sample(xs.reshape(ns, d), ms[1], ms[0], tabs_s, w_proj,
                                                            p["conv_w"], p1, p2, t)
    pad_t = lambda a, rows: jnp.pad(a.reshape(db, t, -1), ((0, 0), (0, rows - t), (0, 0)))
    head_rows = lambda a: pad_t(a, SUBLANES).reshape(db, SUBLANES, -1, HEAD_DIM).transpose(0, 2, 1, 3).reshape(
        db, -1, HEAD_DIM)
    new_cols = lambda a: jnp.pad(a.reshape(db, t, -1).transpose(0, 2, 1), ((0, 0), (0, 0), (0, LANES - t)))
    n_pool = cache_k.shape[0]
    feature_major = lambda c: c.transpose(0, 2, 3, 1).reshape(n_pool, KV_WIDTH, PAGE_SIZE)
    bias_s = _select_sample(page_table, head_rows(qis), pad_t(wis, SUBLANES), new_cols(kis),
                            cache_ki.transpose(0, 2, 1), t)
    att_s = _attn_sample(page_table, head_rows(qs), bias_s, new_cols(ks), new_cols(vs),
                         feature_major(cache_k), feature_major(cache_v))
    att_s = att_s.reshape(db, N_HEADS, SUBLANES, HEAD_DIM)[:, :, :t].transpose(0, 2, 1, 3).reshape(ns, ATT_WIDTH)

    cnt0 = jnp.zeros((1, LANES), f32)
    x1p, h2p, tope_p, comb_p, rank_p, cnt1 = _post(xp, att_p, conv_p, (mp[1], mp[0], mp[2], mp[4], mp[3]),
                                                   post_w, cnt0, alpha, False)
    x1s, h2s, tope_s, comb_s, rank_s, cnt2 = _post(xs.reshape(ns, d), att_s, conv_s, (ms[1], ms[0], ms[2], ms[4], ms[3]),
                                                   post_w, cnt1, alpha, True)

    n_p = b * s
    counts = cnt2[0, :N_EXPERTS].astype(i32)
    padded = (counts + MOE_ROWS - 1) // MOE_ROWS * MOE_ROWS
    pend = jnp.cumsum(padded).astype(i32)
    pstart = pend - padded
    tope = jnp.concatenate([tope_p.reshape(n_p, LANES)[:, :TOP_K], tope_s[:, :TOP_K]], axis=0)
    rank = jnp.concatenate([rank_p.reshape(n_p, LANES)[:, :TOP_K], rank_s[:, :TOP_K]], axis=0)
    pos = (pstart[tope] + rank).astype(i32)
    n_blocks = ((n_p + ns) * TOP_K + N_EXPERTS * (MOE_ROWS - 1)) // MOE_ROWS
    n_used = pend[-1:] // MOE_ROWS
    blk = jnp.minimum(jnp.arange(n_blocks, dtype=i32), n_used[0] - 1)
    block_e = jnp.minimum(jnp.sum(pend[None, :] <= (blk * MOE_ROWS)[:, None], axis=1), N_EXPERTS - 1).astype(i32)
    pos_p, pos_s = pos[:n_p].reshape(-1), pos[n_p:].reshape(-1)

    rows = _dispatch(pos.reshape(-1), pend, h2p.reshape(n_p, d), h2s, n_blocks)
    y = _experts(block_e, n_used, rows, p["w_gate_up"], p["b_gate_up"], p["w_down"], p["b_down"])
    yp = _combine(pos_p, y, x1p, comb_p, mp[5], *ln2, alpha, False)
    ys = _combine(pos_s, y, x1s, comb_s, ms[5], *ln2, alpha, True).reshape(db, t, d)

    token_major = lambda a: a.reshape(b, N_KV_HEADS, HEAD_DIM, s).transpose(0, 3, 1, 2)
    new_p = (token_major(kt), token_major(vt), kit.transpose(0, 2, 1), cst_p)
    new_s = (ks.reshape(db, t, N_KV_HEADS, HEAD_DIM), vs.reshape(db, t, N_KV_HEADS, HEAD_DIM),
             kis.reshape(db, t, IDX_DIM), u_s.reshape(db, t, d // 2)[:, t - (CONV_K - 1):])
    return yp, ys, new_p, new_s


def kernel(x_prompt, x_sample, cache_k, cache_v, cache_idx_k, state_conv, page_table, c_prompt, c_sample,
           w_ada, b_ada, w_in, conv_w, w_branch_a, w_branch_b, w_o, ln1_g, ln1_b, ln2_g, ln2_b,
           w_router, b_router, w_gate_up, b_gate_up, w_down, b_down):
    depth = w_ada.shape[0]
    alpha = float((2 * depth) ** 0.25)
    xp, xs = x_prompt, x_sample
    acc_p, acc_s = [], []
    for l in range(depth):
        p = {"w_ada": w_ada[l], "b_ada": b_ada[l], "w_in": w_in[l], "conv_w": conv_w[l],
             "w_branch_a": w_branch_a[l], "w_branch_b": w_branch_b[l], "w_o": w_o[l],
             "ln1_g": ln1_g[l], "ln1_b": ln1_b[l], "ln2_g": ln2_g[l], "ln2_b": ln2_b[l],
             "w_router": w_router[l], "b_router": b_router[l], "w_gate_up": w_gate_up[l],
             "b_gate_up": b_gate_up[l], "w_down": w_down[l], "b_down": b_down[l]}
        xp, xs, new_p, new_s = _layer(xp, xs, c_prompt, c_sample, cache_k[l], cache_v[l], cache_idx_k[l],
                                      state_conv[l], page_table, p, alpha)
        acc_p.append(new_p)
        acc_s.append(new_s)
    stack = lambda acc, j: jnp.stack([a[j] for a in acc])
    return (xp, xs, stack(acc_p, 0), stack(acc_p, 1), stack(acc_p, 2), stack(acc_p, 3),
            stack(acc_s, 0), stack(acc_s, 1), stack(acc_s, 2), stack(acc_s, 3))
```

```python
import functools

import jax
import jax.numpy as jnp
import numpy as np
from jax import lax
from jax.experimental import pallas as pl
from jax.experimental.pallas import tpu as pltpu

f32 = jnp.float32
bf16 = jnp.bfloat16
i32 = jnp.int32

N_HEADS = 8
HEAD_DIM = 64
N_KV_HEADS = 4
KV_GROUPS = N_HEADS // N_KV_HEADS
ATT_WIDTH = N_HEADS * HEAD_DIM
KV_WIDTH = N_KV_HEADS * HEAD_DIM
ROPE_DIM = HEAD_DIM // 4
ROPE_HALF = ROPE_DIM // 2
ROPE_THETA = 500000.0
IDX_HEADS = 8
IDX_DIM = 64
IDX_W_SCALE = (IDX_HEADS * IDX_DIM) ** -0.5
TOPK_MAX = 256
PAGE_SIZE = 128
CONV_K = 3
N_EXPERTS = 32
TOP_K = 4
SWIGLU_LIMIT = 7.0
SWIGLU_ALPHA = 1.702
LN_EPS = 1e-5

LANES = 128
SUBLANES = 8
ROW_CHUNKS = SUBLANES
VMEM_LIMIT = 56 * 1024 * 1024

ROW_TILE = 256
POST_TILE = 512
MOE_ROWS = 256
COMBINE_TILE = 128
Q_BLOCK = 128
KEY_CLASS = 1024
QK_SCALE = HEAD_DIM ** -0.5
SCORE_CHUNK = 512
SEQ_GROUP = 8
HALF_ROWS = 2 * SUBLANES
BISECT_UNROLL = 4
i16 = jnp.int16
INT_MIN = int(np.iinfo(np.int32).min)
NEG_BIG = -1e30


def _cparams(sem):
    return pltpu.CompilerParams(dimension_semantics=sem, vmem_limit_bytes=VMEM_LIMIT)


def _mm(a, b):
    return jnp.dot(a.astype(bf16), b.astype(bf16), preferred_element_type=f32)


def _mm_nt(a, b):
    return lax.dot_general(a.astype(bf16), b.astype(bf16), (((1,), (1,)), ((), ())), preferred_element_type=f32)


def _store_row_tiles(ref, r0, value):
    n = value.shape[0]
    for j in range(ROW_CHUNKS):
        ref[pl.ds(r0 * ROW_CHUNKS + j, n, stride=ROW_CHUNKS), :] = value[:, j * LANES:(j + 1) * LANES]


def _load_row_tiles(ref, n, *lead):
    return jnp.concatenate([ref[lead + (pl.ds(j, n, stride=ROW_CHUNKS), slice(None))] for j in range(ROW_CHUNKS)],
                           axis=1)


def _row_tile(ref, r, *lead):
    start = r * ROW_CHUNKS if isinstance(r, int) else pl.multiple_of(r * ROW_CHUNKS, ROW_CHUNKS)
    return ref.at[lead + (pl.ds(start, ROW_CHUNKS), slice(None))]


def _layer_norm(y, g, b):
    mu = jnp.mean(y, axis=-1, keepdims=True)
    d = y - mu
    var = jnp.mean(d * d, axis=-1, keepdims=True)
    return d * lax.rsqrt(var + LN_EPS) * g + b


def _ada_kernel(c_ref, w_ref, b_ref, o_ref):
    o_ref[...] = _mm(c_ref[...], w_ref[...]) + b_ref[...]


def _ada(c_all, w_ada, b_ada):
    n, d = c_all.shape
    width = w_ada.shape[1]
    return pl.pallas_call(
        _ada_kernel,
        grid=(width // d,),
        in_specs=[pl.BlockSpec((n, d), lambda j: (0, 0)),
                  pl.BlockSpec((d, d), lambda j: (0, j)),
                  pl.BlockSpec((1, d), lambda j: (0, j))],
        out_specs=pl.BlockSpec((n, d), lambda j: (0, j)),
        out_shape=jax.ShapeDtypeStruct((n, width), f32),
        compiler_params=_cparams(("arbitrary",)),
        name="ada",
    )(c_all, w_ada, b_ada.reshape(1, width))


_SEG = {}
_off = 0
for _name, _w in (("q", ATT_WIDTH), ("k", KV_WIDTH), ("v", KV_WIDTH), ("qi", IDX_HEADS * IDX_DIM),
                  ("ki", LANES), ("wi", LANES), ("xin", 512), ("bg", 512), ("cg", 512)):
    _SEG[_name] = (_off, _off + _w)
    _off += _w
PROJ_WIDTH = _off


def _rope(t, c, s1, s2):
    w = t.shape[1]
    reps = w // LANES
    if reps > 1:
        c, s1, s2 = (jnp.tile(a, (1, reps)) for a in (c, s1, s2))
    return t * c + pltpu.roll(t, w - ROPE_HALF, 1) * s1 + pltpu.roll(t, ROPE_HALF, 1) * s2


def _project(x_ref, sc_ref, sh_ref, cos_ref, s1_ref, s2_ref, w_ref):
    h = x_ref[...] * (1.0 + sc_ref[...]) + sh_ref[...]
    hb = h.astype(bf16)
    c, s1, s2 = cos_ref[...], s1_ref[...], s2_ref[...]

    def seg(name):
        a, b = _SEG[name]
        return jnp.dot(hb, w_ref[:, a:b], preferred_element_type=f32)

    q = _rope(seg("q"), c, s1, s2)
    k = _rope(seg("k"), c, s1, s2)
    v = seg("v")
    qi = _rope(seg("qi"), c, s1, s2)
    ki = _rope(seg("ki"), c, s1, s2)[:, :IDX_DIM]
    wi = seg("wi")
    u = seg("cg") * seg("xin")
    return q, k, v, qi, ki, wi, u, seg("bg")


def _conv_out(u, bg, cw_ref, sh1, sh2):
    y = cw_ref[0:1, :] * sh2 + cw_ref[1:2, :] * sh1
    y = y + cw_ref[2:3, :] * u
    return bg * y


def _rope_rows(t, cos_t, sin_t, n_heads):
    parts = []
    for g in range(n_heads):
        base = g * HEAD_DIM
        r0, r1 = t[base:base + ROPE_HALF], t[base + ROPE_HALF:base + ROPE_DIM]
        parts += [r0 * cos_t - r1 * sin_t, r1 * cos_t + r0 * sin_t, t[base + ROPE_DIM:base + HEAD_DIM]]
    return jnp.concatenate(parts, axis=0)


def _inproj_prompt_kernel(x_ref, sc_ref, sh_ref, cos_ref, s1_ref, s2_ref, cost_ref, sint_ref, w_ref, wt_ref, cw_ref,
                          q_ref, qi_ref, wi_ref, kt_ref, vt_ref, kit_ref, ktb_ref, vtb_ref, kitb_ref,
                          conv_ref, cst_ref, carry_ref):
    @pl.when(pl.program_id(1) == 0)
    def _():
        carry_ref[...] = jnp.zeros_like(carry_ref)

    hb = (x_ref[...] * (1.0 + sc_ref[...]) + sh_ref[...]).astype(bf16)
    c, s1, s2 = cos_ref[...], s1_ref[...], s2_ref[...]

    def seg(name):
        a, b = _SEG[name]
        return jnp.dot(hb, w_ref[:, a:b], preferred_element_type=f32)

    q = _rope(seg("q"), c, s1, s2) * QK_SCALE
    qi = _rope(seg("qi"), c, s1, s2)
    for h in range(N_HEADS):
        q_ref[h] = q[:, h * HEAD_DIM:(h + 1) * HEAD_DIM].astype(bf16)
    for h in range(IDX_HEADS):
        qi_ref[h] = qi[:, h * IDX_DIM:(h + 1) * IDX_DIM].astype(bf16)
    wi_ref[...] = seg("wi") * IDX_W_SCALE

    t = lax.dot_general(wt_ref[...], hb, (((1,), (1,)), ((), ())), preferred_element_type=f32)
    cos_t, sin_t = cost_ref[...], sint_ref[...]
    kt = _rope_rows(t[0:KV_WIDTH], cos_t, sin_t, N_KV_HEADS)
    vt = t[KV_WIDTH:2 * KV_WIDTH]
    kit = _rope_rows(t[2 * KV_WIDTH:2 * KV_WIDTH + IDX_DIM], cos_t, sin_t, 1)
    kt_ref[...], vt_ref[...], kit_ref[...] = kt, vt, kit
    ktb_ref[...], vtb_ref[...], kitb_ref[...] = kt.astype(bf16), vt.astype(bf16), kit.astype(bf16)

    u = seg("cg") * seg("xin")
    bg = seg("bg")
    tm = u.shape[0]
    rows = lax.broadcasted_iota(i32, u.shape, 0)
    c0, c1 = carry_ref[0:1, :], carry_ref[1:2, :]
    sh1 = jnp.where(rows == 0, c1, pltpu.roll(u, 1, 0))
    sh2 = jnp.where(rows == 0, c0, jnp.where(rows == 1, c1, pltpu.roll(u, 2, 0)))
    conv_ref[...] = _conv_out(u, bg, cw_ref, sh1, sh2)
    tail = u[tm - 2:tm, :]
    carry_ref[0:2, :] = tail
    cst_ref[...] = tail


def _inproj_sample_kernel(x_ref, sc_ref, sh_ref, cos_ref, s1_ref, s2_ref, w_ref, cw_ref, p1_ref, p2_ref,
                          q_ref, k_ref, v_ref, qi_ref, ki_ref, wi_ref, conv_ref, u_ref, *, dec_seq):
    q, k, v, qi, ki, wi, u, bg = _project(x_ref, sc_ref, sh_ref, cos_ref, s1_ref, s2_ref, w_ref)
    q_ref[...], k_ref[...], v_ref[...], qi_ref[...], ki_ref[...], wi_ref[...] = q, k, v, qi, ki, wi
    t = lax.broadcasted_iota(i32, u.shape, 0) % dec_seq
    sh1 = jnp.where(t == 0, p1_ref[...], pltpu.roll(u, 1, 0))
    sh2 = jnp.where(t < 2, p2_ref[...], pltpu.roll(u, 2, 0))
    conv_ref[...] = _conv_out(u, bg, cw_ref, sh1, sh2)
    u_ref[...] = u


def _inproj_prompt(x, sc1, sh1, tabs, tabs_t, w_proj, w_t, conv_w):
    b, s, d = x.shape
    tm = min(ROW_TILE, s)
    row = lambda w: pl.BlockSpec((None, tm, w), lambda bi, i: (bi, i, 0))
    heads = lambda n: pl.BlockSpec((None, n, tm, HEAD_DIM), lambda bi, i: (bi, 0, i, 0))
    col = lambda r: pl.BlockSpec((None, r, tm), lambda bi, i: (bi, 0, i))
    per_b = pl.BlockSpec((None, 1, d), lambda bi, i: (bi, 0, 0))
    tab = pl.BlockSpec((tm, LANES), lambda bi, i: (i, 0))
    tab_t = pl.BlockSpec((ROPE_HALF, tm), lambda bi, i: (0, i))
    const = lambda a: pl.BlockSpec(a.shape, lambda bi, i: (0,) * a.ndim)
    feat = (KV_WIDTH, KV_WIDTH, IDX_DIM)
    return pl.pallas_call(
        _inproj_prompt_kernel,
        grid=(b, s // tm),
        in_specs=[row(d), per_b, per_b, tab, tab, tab, tab_t, tab_t, const(w_proj), const(w_t), const(conv_w)],
        out_specs=[heads(N_HEADS), heads(IDX_HEADS), row(LANES)] + [col(r) for r in feat] + [col(r) for r in feat]
        + [row(512), pl.BlockSpec((None, CONV_K - 1, 512), lambda bi, i: (bi, 0, 0))],
        out_shape=[jax.ShapeDtypeStruct((b, N_HEADS, s, HEAD_DIM), bf16),
                   jax.ShapeDtypeStruct((b, IDX_HEADS, s, IDX_DIM), bf16),
                   jax.ShapeDtypeStruct((b, s, LANES), f32)]
        + [jax.ShapeDtypeStruct((b, r, s), f32) for r in feat]
        + [jax.ShapeDtypeStruct((b, r, s), bf16) for r in feat]
        + [jax.ShapeDtypeStruct((b, s, 512), f32), jax.ShapeDtypeStruct((b, CONV_K - 1, 512), f32)],
        scratch_shapes=[pltpu.VMEM((SUBLANES, 512), f32)],
        compiler_params=_cparams(("arbitrary", "arbitrary")),
        name="inproj_prompt",
    )(x, sc1, sh1, *tabs, *tabs_t, w_proj, w_t, conv_w)


def _inproj_sample(x, sc1, sh1, tabs, w_proj, conv_w, p1, p2, dec_seq):
    n, d = x.shape
    widths = (ATT_WIDTH, KV_WIDTH, KV_WIDTH, IDX_HEADS * IDX_DIM, IDX_DIM, LANES, 512, 512)
    full = lambda r, w: pl.BlockSpec((r, w), lambda i: (0, 0))
    return pl.pallas_call(
        functools.partial(_inproj_sample_kernel, dec_seq=dec_seq),
        grid=(1,),
        in_specs=[full(n, d), full(n, d), full(n, d), full(n, LANES), full(n, LANES), full(n, LANES),
                  full(d, PROJ_WIDTH), full(CONV_K, 512), full(n, 512), full(n, 512)],
        out_specs=[full(n, w) for w in widths],
        out_shape=[jax.ShapeDtypeStruct((n, w), f32) for w in widths],
        compiler_params=_cparams(("arbitrary",)),
        name="inproj_sample",
    )(x, sc1, sh1, *tabs, w_proj, conv_w, p1, p2)


def _sort_key(score):
    bits = pltpu.bitcast(jnp.where(score == 0.0, 0.0, score), i32)
    return bits ^ ((bits >> 31) & jnp.int32(0x7FFFFFFF))


def _bisect16(ref, n_cols, need):
    n_groups = ref.shape[0] // HALF_ROWS
    one, zero = jnp.ones((), i16), jnp.zeros((), i16)

    def refine(bit, ts):
        out = []
        for g in range(n_groups):
            rows = slice(g * HALF_ROWS, (g + 1) * HALF_ROWS)
            cand = ts[g] + jnp.left_shift(jnp.int32(1), bit)
            cand16 = cand.astype(i16)
            acc = jnp.zeros((HALF_ROWS, LANES), i16)
            for c in range(n_cols // LANES):
                acc = acc + jnp.where(ref[rows, c * LANES:(c + 1) * LANES] >= cand16, one, zero)
            cnt = jnp.sum(acc.astype(f32), axis=1, keepdims=True)
            out.append(jnp.where(cnt >= need[rows], cand, ts[g]))
        return tuple(out)

    def outer(i, ts):
        for j in range(BISECT_UNROLL):
            ts = refine(15 - (i * BISECT_UNROLL + j), ts)
        return ts

    ts = lax.fori_loop(0, 16 // BISECT_UNROLL, outer,
                       tuple(jnp.full((HALF_ROWS, 1), -32768, i32) for _ in range(n_groups)))
    return jnp.concatenate(ts, axis=0)


def _select_topk(key_ref, hi_ref, lo_ref, n_cols, topk):
    n_rows = key_ref.shape[0]
    kf = float(topk)

    def count(pred):
        return jnp.sum(jnp.where(pred, 1.0, 0.0), axis=1, keepdims=True)

    for c0 in range(0, n_cols, SCORE_CHUNK):
        cols = slice(c0, min(c0 + SCORE_CHUNK, n_cols))
        k = key_ref[:, cols]
        hi_ref[:, cols] = (k >> 16).astype(i16)
        lo_ref[:, cols] = ((k & 0xFFFF) - 32768).astype(i16)
    t_hi = _bisect16(hi_ref, n_cols, jnp.full((n_rows, 1), kf, f32))
    t_hi16 = t_hi.astype(i16)
    acc = jnp.zeros((n_rows, LANES), i16)
    for c in range(n_cols // LANES):
        cols = slice(c * LANES, (c + 1) * LANES)
        h = hi_ref[:, cols]
        acc = acc + jnp.where(h > t_hi16, jnp.ones((), i16), jnp.zeros((), i16))
        lo_ref[:, cols] = jnp.where(h == t_hi16, lo_ref[:, cols], jnp.full((), -32768, i16))
    above = jnp.sum(acc.astype(f32), axis=1, keepdims=True)
    t_lo = _bisect16(lo_ref, n_cols, kf - above)
    t = jnp.left_shift(t_hi, 16) + (t_lo + 32768)
    keys = key_ref[:, 0:n_cols]
    n_ge = count(keys >= t)
    col = lax.broadcasted_iota(i32, (n_rows, n_cols), 1)
    n_bits = int(n_cols).bit_length()

    def tie_search(_):
        need = kf - count(key_ref[:, 0:n_cols] > t)

        def tstep(i, j):
            cand = j + jnp.left_shift(jnp.int32(1), n_bits - 1 - i)
            below = count((key_ref[:, 0:n_cols] == t) & (col < cand))
            return jnp.where(below < need, cand, j)

        return lax.fori_loop(0, n_bits, tstep, jnp.zeros((n_rows, 1), i32))

    jmax = lax.cond(jnp.max(n_ge) > kf, tie_search, lambda _: jnp.full((n_rows, 1), n_cols, i32), 0)
    return (keys > INT_MIN) & ((keys > t) | ((keys == t) & (col <= jmax)))


def _masked_attention(q, kt, vt, bias):
    logits = jnp.dot(q, kt, preferred_element_type=f32) + bias
    m = jnp.max(logits, axis=1, keepdims=True)
    p = jnp.exp(logits - m)
    den = jnp.sum(p, axis=1, keepdims=True)
    o = lax.dot_general(p.astype(bf16), vt, (((1,), (1,)), ((), ())), preferred_element_type=f32)
    return o / den


def _index_scores(qi_rows, wi, kit, n_heads_rows):
    dots = jnp.dot(qi_rows, kit, preferred_element_type=f32)
    r = n_heads_rows
    score = wi[:, 0:1] * jnp.maximum(dots[0:r], 0.0)
    for h in range(1, IDX_HEADS):
        score = score + wi[:, h:h + 1] * jnp.maximum(dots[h * r:(h + 1) * r], 0.0)
    return score


def _attn_prompt_body(qi_ref, wi_ref, q_ref, kit_ref, kt_ref, vt_ref, o_ref, key_ref, hi_ref, lo_ref,
                      s0, n_keys, topk):
    qi_rows = qi_ref[...].reshape(IDX_HEADS * Q_BLOCK, IDX_DIM)
    wi = wi_ref[...]
    chunk = min(SCORE_CHUNK, n_keys)

    for c0 in range(0, n_keys, chunk):
        score = _index_scores(qi_rows, wi, kit_ref[:, c0:c0 + chunk], Q_BLOCK)
        col = c0 + lax.broadcasted_iota(i32, (Q_BLOCK, chunk), 1)
        qpos = s0 + lax.broadcasted_iota(i32, (Q_BLOCK, chunk), 0)
        key_ref[:, c0:c0 + chunk] = jnp.where(col <= qpos, _sort_key(score), INT_MIN)
    bias = jnp.where(_select_topk(key_ref, hi_ref, lo_ref, n_keys, topk), 0.0, -jnp.inf)
    for h in range(N_HEADS):
        g = h // KV_GROUPS
        o = _masked_attention(q_ref[h], kt_ref[g * HEAD_DIM:(g + 1) * HEAD_DIM, 0:n_keys],
                              vt_ref[g * HEAD_DIM:(g + 1) * HEAD_DIM, 0:n_keys], bias)
        o_ref[:, h * HEAD_DIM:(h + 1) * HEAD_DIM] = o


def _attn_prompt_kernel(qi_ref, wi_ref, q_ref, kit_ref, kt_ref, vt_ref, o_ref, key_ref, hi_ref, lo_ref,
                        *, key_class, topk):
    s0 = pl.program_id(1) * Q_BLOCK
    n_classes = kit_ref.shape[1] // key_class
    cls = (s0 + Q_BLOCK - 1) // key_class
    for c in range(n_classes):
        @pl.when(cls == c)
        def _(c=c):
            _attn_prompt_body(qi_ref, wi_ref, q_ref, kit_ref, kt_ref, vt_ref, o_ref, key_ref, hi_ref, lo_ref,
                              s0, (c + 1) * key_class, topk)


def _attn_prompt(qi, wi, q, kit, kt, vt):
    b, _, s, _ = q.shape
    key_class = min(KEY_CLASS, s)
    topk = min(TOPK_MAX, s // 4)
    blk = lambda w: pl.BlockSpec((None, Q_BLOCK, w), lambda bi, i: (bi, i, 0))
    heads = lambda n: pl.BlockSpec((None, n, Q_BLOCK, HEAD_DIM), lambda bi, i: (bi, 0, i, 0))
    whole = lambda r: pl.BlockSpec((None, r, s), lambda bi, i: (bi, 0, 0))
    return pl.pallas_call(
        functools.partial(_attn_prompt_kernel, key_class=key_class, topk=topk),
        grid=(b, s // Q_BLOCK),
        in_specs=[heads(IDX_HEADS), blk(LANES), heads(N_HEADS), whole(IDX_DIM), whole(KV_WIDTH), whole(KV_WIDTH)],
        out_specs=blk(ATT_WIDTH),
        out_shape=jax.ShapeDtypeStruct((b, s, ATT_WIDTH), f32),
        scratch_shapes=[pltpu.VMEM((Q_BLOCK, s), i32), pltpu.VMEM((Q_BLOCK, s), i16), pltpu.VMEM((Q_BLOCK, s), i16)],
        compiler_params=_cparams(("arbitrary", "arbitrary")),
        name="attn_prompt",
    )(qi, wi, q, kit, kt, vt)


def _select_sample_kernel(pt_ref, qi_ref, wi_ref, kin_ref, cki_ref, bias_ref, kit_buf, key_ref, hi_ref, lo_ref, sem,
                          *, n_pages, dec_seq, topk, group):
    step = pl.program_id(0)
    past = n_pages * PAGE_SIZE
    n_keys = past + LANES

    def copy(j):
        s, p = j // n_pages, j % n_pages
        page = pt_ref[(step * group + s) * n_pages + p]
        cols = pl.ds(pl.multiple_of(p * PAGE_SIZE, PAGE_SIZE), PAGE_SIZE)
        return pltpu.make_async_copy(cki_ref.at[page], kit_buf.at[s, :, cols], sem)

    lax.fori_loop(0, group * n_pages, lambda j, c: (copy(j).start(), c)[1], 0)
    kit_buf[:, :, past:n_keys] = kin_ref[...]
    lax.fori_loop(0, group * n_pages, lambda j, c: (copy(j).wait(), c)[1], 0)

    col = lax.broadcasted_iota(i32, (SUBLANES, n_keys), 1)
    tok = lax.broadcasted_iota(i32, (SUBLANES, n_keys), 0)
    allowed = (col < past) | ((col - past <= tok) & (col - past < dec_seq))
    for s in range(group):
        score = _index_scores(qi_ref[s].astype(bf16), wi_ref[s] * IDX_W_SCALE, kit_buf[s].astype(bf16), SUBLANES)
        key_ref[s * SUBLANES:(s + 1) * SUBLANES, :] = jnp.where(allowed, _sort_key(score), INT_MIN)
    bias = jnp.where(_select_topk(key_ref, hi_ref, lo_ref, n_keys, topk), 0.0, -jnp.inf)
    for s in range(group):
        bias_ref[s] = bias[s * SUBLANES:(s + 1) * SUBLANES, :]


def _select_sample(page_table, qi_rows, wi8, kit_new, cache_kit, dec_seq):
    db, n_pages = page_table.shape
    n_keys = n_pages * PAGE_SIZE + LANES
    topk = min(TOPK_MAX, (n_pages * PAGE_SIZE + dec_seq) // 4)
    group = min(SEQ_GROUP, db)
    per = lambda r, w: pl.BlockSpec((group, r, w), lambda i, pt: (i, 0, 0))
    return pl.pallas_call(
        functools.partial(_select_sample_kernel, n_pages=n_pages, dec_seq=dec_seq, topk=topk, group=group),
        grid_spec=pltpu.PrefetchScalarGridSpec(
            num_scalar_prefetch=1, grid=(db // group,),
            in_specs=[per(IDX_HEADS * SUBLANES, IDX_DIM), per(SUBLANES, LANES), per(IDX_DIM, LANES),
                      pl.BlockSpec(memory_space=pl.ANY)],
            out_specs=per(SUBLANES, n_keys),
            scratch_shapes=[pltpu.VMEM((group, IDX_DIM, n_keys), f32), pltpu.VMEM((group * SUBLANES, n_keys), i32),
                            pltpu.VMEM((group * SUBLANES, n_keys), i16), pltpu.VMEM((group * SUBLANES, n_keys), i16),
                            pltpu.SemaphoreType.DMA(())]),
        out_shape=jax.ShapeDtypeStruct((db, SUBLANES, n_keys), f32),
        compiler_params=_cparams(("arbitrary",)),
        name="select_sample",
    )(page_table.reshape(-1), qi_rows, wi8, kit_new, cache_kit)


def _attn_sample_kernel(pt_ref, q_ref, bias_ref, ktn_ref, vtn_ref, ckt_ref, cvt_ref, o_ref, kt_buf, vt_buf, sem,
                        *, n_pages, n_seqs):
    b = pl.program_id(0)
    past = n_pages * PAGE_SIZE
    n_keys = past + LANES

    def pages(seq, slot, wait):
        def body(p, c):
            page = pt_ref[seq * n_pages + p]
            cols = pl.ds(pl.multiple_of(p * PAGE_SIZE, PAGE_SIZE), PAGE_SIZE)
            for src, dst, j in ((ckt_ref, kt_buf, 0), (cvt_ref, vt_buf, 1)):
                cp = pltpu.make_async_copy(src.at[page], dst.at[slot, :, cols], sem.at[j, slot])
                if wait:
                    cp.wait()
                else:
                    cp.start()
            return c

        lax.fori_loop(0, n_pages, body, 0)

    @pl.when(b == 0)
    def _():
        pages(0, 0, False)

    @pl.when(b + 1 < n_seqs)
    def _():
        pages(b + 1, (b + 1) % 2, False)

    slot = b % 2
    kt_buf[slot, :, past:n_keys] = ktn_ref[...]
    vt_buf[slot, :, past:n_keys] = vtn_ref[...]
    pages(b, slot, True)

    bias = bias_ref[...]
    bias2 = jnp.concatenate([bias] * KV_GROUPS, axis=0)
    rows = KV_GROUPS * SUBLANES
    for g in range(N_KV_HEADS):
        kt = kt_buf[slot, g * HEAD_DIM:(g + 1) * HEAD_DIM, :].astype(bf16)
        vt = vt_buf[slot, g * HEAD_DIM:(g + 1) * HEAD_DIM, :].astype(bf16)
        q = (q_ref[g * rows:(g + 1) * rows, :] * QK_SCALE).astype(bf16)
        o_ref[g * rows:(g + 1) * rows, :] = _masked_attention(q, kt, vt, bias2)


def _attn_sample(page_table, q_rows, bias, kt_new, vt_new, cache_kt, cache_vt):
    db, n_pages = page_table.shape
    n_keys = n_pages * PAGE_SIZE + LANES
    per = lambda r, w: pl.BlockSpec((None, r, w), lambda bi, pt: (bi, 0, 0))
    hbm = pl.BlockSpec(memory_space=pl.ANY)
    return pl.pallas_call(
        functools.partial(_attn_sample_kernel, n_pages=n_pages, n_seqs=db),
        grid_spec=pltpu.PrefetchScalarGridSpec(
            num_scalar_prefetch=1, grid=(db,),
            in_specs=[per(N_HEADS * SUBLANES, HEAD_DIM), per(SUBLANES, n_keys), per(KV_WIDTH, LANES),
                      per(KV_WIDTH, LANES), hbm, hbm],
            out_specs=per(N_HEADS * SUBLANES, HEAD_DIM),
            scratch_shapes=[pltpu.VMEM((2, KV_WIDTH, n_keys), f32), pltpu.VMEM((2, KV_WIDTH, n_keys), f32),
                            pltpu.SemaphoreType.DMA((2, 2))]),
        out_shape=jax.ShapeDtypeStruct((db, N_HEADS * SUBLANES, HEAD_DIM), f32),
        compiler_params=_cparams(("arbitrary",)),
        name="attn_sample",
    )(page_table.reshape(-1), q_rows, bias, kt_new, vt_new, cache_kt, cache_vt)


def _post_body(x_ref, att_ref, conv_ref, sc1_ref, sh1_ref, g1_ref, sc2_ref, sh2_ref,
               wg_ref, wa_ref, wb_ref, wo_ref, wr_ref, br_ref, lng_ref, lnb_ref, cnt_in_ref,
               x1_ref, h2_ref, tope_ref, comb_ref, rank_ref, cnt_ref, *, alpha):
    del cnt_in_ref
    n = x_ref.shape[0]
    sub = min(ROW_TILE, n)
    cnt = cnt_ref[...]
    for r0 in range(0, n, sub):
        cnt = _post_rows(slice(r0, r0 + sub), cnt, x_ref, att_ref, conv_ref, sc1_ref, sh1_ref, g1_ref, sc2_ref,
                         sh2_ref, wg_ref, wa_ref, wb_ref, wo_ref, wr_ref, br_ref, lng_ref, lnb_ref,
                         x1_ref, h2_ref, tope_ref, comb_ref, rank_ref, alpha)
    cnt_ref[...] = cnt


def _post_rows(rows, cnt, x_ref, att_ref, conv_ref, sc1_ref, sh1_ref, g1_ref, sc2_ref, sh2_ref,
               wg_ref, wa_ref, wb_ref, wo_ref, wr_ref, br_ref, lng_ref, lnb_ref,
               x1_ref, h2_ref, tope_ref, comb_ref, rank_ref, alpha):
    mod = lambda ref: ref[...] if ref.shape[0] == 1 else ref[rows, :]
    d = x_ref.shape[-1]
    x = x_ref[rows, :]
    hb = (x * (1.0 + mod(sc1_ref)) + mod(sh1_ref)).astype(bf16)
    ga = jnp.dot(hb, wg_ref[:, 0:d], preferred_element_type=f32)
    gb = jnp.dot(hb, wg_ref[:, d:2 * d], preferred_element_type=f32)
    a = _mm(att_ref[rows, :], wa_ref[...])
    c = _mm(conv_ref[rows, :], wb_ref[...])
    mix = _mm(jax.nn.sigmoid(ga) * a + jax.nn.sigmoid(gb) * c, wo_ref[...])
    x1 = _layer_norm(alpha * x + mod(g1_ref) * mix, lng_ref[...], lnb_ref[...])
    x1_ref[rows, :] = x1
    h2 = x1 * (1.0 + mod(sc2_ref)) + mod(sh2_ref)
    _store_row_tiles(h2_ref, rows.start, h2)
    logits =_mm(h2, wr_ref[...]) + br_ref[...]

    tm = logits.shape[0]
    lane = lax.broadcasted_iota(i32, (tm, LANES), 1)
    tope = jnp.zeros((tm, LANES), i32)
    topv = jnp.full((tm, LANES), -jnp.inf, f32)
    hot = jnp.zeros((tm, LANES), f32)
    picks = []
    work = logits
    for k in range(TOP_K):
        m = jnp.max(work, axis=1, keepdims=True)
        idx = jnp.min(jnp.where(work == m, lane, LANES), axis=1, keepdims=True)
        hit = lane == idx
        tope = jnp.where(lane == k, idx, tope)
        topv = jnp.where(lane == k, m, topv)
        hot = hot + jnp.where(hit, 1.0, 0.0)
        work = jnp.where(hit, -jnp.inf, work)
        picks.append(idx)
    e = jnp.where(lane < TOP_K, jnp.exp(topv - jnp.max(topv, axis=1, keepdims=True)), 0.0)
    comb_ref[rows, :] = e / jnp.sum(e, axis=1, keepdims=True)
    tope_ref[rows, :] = tope

    r_i = lax.broadcasted_iota(i32, (tm, tm), 0)
    c_i = lax.broadcasted_iota(i32, (tm, tm), 1)
    lower = jnp.where(c_i < r_i, 1.0, 0.0)
    before = _mm(lower, hot) + cnt
    rank = jnp.zeros((tm, LANES), f32)
    for k in range(TOP_K):
        rk = jnp.sum(jnp.where(lane == picks[k], before, 0.0), axis=1, keepdims=True)
        rank = jnp.where(lane == k, rk, rank)
    rank_ref[rows, :] = rank.astype(i32)
    return cnt + jnp.sum(hot, axis=0, keepdims=True)


def _post(x, att, conv, mods, weights, cnt_in, alpha, per_row_mods):
    wg, wa, wb, wo, wr, br, lng, lnb = weights
    d = x.shape[-1]
    if per_row_mods:
        n = x.shape[0]
        grid = (1,)
        row = lambda w: pl.BlockSpec((n, w), lambda i: (0, 0))
        row_tiles = pl.BlockSpec((n * ROW_CHUNKS, LANES), lambda i: (0, 0))
        mod = row(d)
        const = lambda shp: pl.BlockSpec(shp, lambda i: (0,) * len(shp))
        lead = (n,)
        first = lambda: pl.program_id(0) == 0
        sem = ("arbitrary",)
    else:
        b, s, _ = x.shape
        tm = min(POST_TILE, s)
        grid = (b, s // tm)
        row = lambda w: pl.BlockSpec((None, tm, w), lambda bi, i: (bi, i, 0))
        row_tiles = pl.BlockSpec((None, tm * ROW_CHUNKS, LANES), lambda bi, i: (bi, i, 0))
        mod = pl.BlockSpec((None, 1, d), lambda bi, i: (bi, 0, 0))
        const = lambda shp: pl.BlockSpec(shp, lambda bi, i: (0,) * len(shp))
        lead = (b, s)
        first = lambda: (pl.program_id(0) == 0) & (pl.program_id(1) == 0)
        sem = ("arbitrary", "arbitrary")

    def body(*refs):
        cnt_in_ref, cnt_ref = refs[16], refs[22]

        @pl.when(first())
        def _():
            cnt_ref[...] = cnt_in_ref[...]

        _post_body(*refs, alpha=alpha)

    assert d == ROW_CHUNKS * LANES
    tiles = lead[:-1] + (lead[-1] * ROW_CHUNKS, LANES)
    outs = [jax.ShapeDtypeStruct(lead + (d,), f32), jax.ShapeDtypeStruct(tiles, f32),
            jax.ShapeDtypeStruct(lead + (LANES,), i32), jax.ShapeDtypeStruct(lead + (LANES,), f32),
            jax.ShapeDtypeStruct(lead + (LANES,), i32), jax.ShapeDtypeStruct((1, LANES), f32)]
    return pl.pallas_call(
        body,
        grid=grid,
        in_specs=[row(d), row(ATT_WIDTH), row(512), mod, mod, mod, mod, mod,
                  const(wg.shape), const(wa.shape), const(wb.shape), const(wo.shape), const(wr.shape),
                  const(br.shape), const(lng.shape), const(lnb.shape), const((1, LANES))],
        out_specs=[row(d), row_tiles, row(LANES), row(LANES), row(LANES), const((1, LANES))],
        out_shape=outs,
        compiler_params=_cparams(sem),
        name="post_rows" if per_row_mods else "post_prompt",
    )(x, att, conv, *mods, wg, wa, wb, wo, wr, br, lng, lnb, cnt_in)


def _dispatch_kernel(pos_ref, pend_ref, hp_ref, hs_ref, xs_ref, zero_ref, sem, zsem, *, n_prompt_steps, n_blocks):
    tm = hp_ref.shape[0] // ROW_CHUNKS
    block_rows = MOE_ROWS * ROW_CHUNKS
    step = pl.program_id(0)

    @pl.when(step == 0)
    def _():
        zero_ref[...] = jnp.zeros_like(zero_ref)

        def zero_block(row0, wait):
            dst = xs_ref.at[pl.ds(pl.multiple_of(row0 * ROW_CHUNKS, block_rows), block_rows), :]
            cp = pltpu.make_async_copy(zero_ref, dst, zsem)
            if wait:
                cp.wait()
            else:
                cp.start()

        def fill(wait):
            def last_block(e, c):
                end = pend_ref[e]
                begin = jnp.where(e == 0, 0, pend_ref[jnp.maximum(e - 1, 0)])

                @pl.when(end > begin)
                def _():
                    zero_block(end - MOE_ROWS, wait)

                return c

            lax.fori_loop(0, N_EXPERTS, last_block, 0)
            n_used = pend_ref[N_EXPERTS - 1] // MOE_ROWS
            lax.fori_loop(n_used, n_blocks, lambda blk, c: (zero_block(blk * MOE_ROWS, wait), c)[1], 0)

        fill(False)
        fill(True)

    def scatter(src_ref, base):
        def issue(r, c):
            for k in range(TOP_K):
                p = pos_ref[(base + r) * TOP_K + k]
                pltpu.make_async_copy(_row_tile(src_ref, r), _row_tile(xs_ref, p), sem).start()
            return c

        def drain(r, c):
            for k in range(TOP_K):
                pltpu.make_async_copy(_row_tile(src_ref, 0), _row_tile(xs_ref, 0), sem).wait()
            return c

        lax.fori_loop(0, src_ref.shape[0] // ROW_CHUNKS, issue, 0)
        lax.fori_loop(0, src_ref.shape[0] // ROW_CHUNKS, drain, 0)

    @pl.when(step < n_prompt_steps)
    def _():
        scatter(hp_ref, step * tm)

    @pl.when(step == n_prompt_steps)
    def _():
        scatter(hs_ref, n_prompt_steps * tm)


def _dispatch(pos, pend, h_prompt, h_sample, n_blocks):
    n = h_prompt.shape[0] // ROW_CHUNKS
    tm = min(ROW_TILE, n)
    steps = n // tm
    return pl.pallas_call(
        functools.partial(_dispatch_kernel, n_prompt_steps=steps, n_blocks=n_blocks),
        grid_spec=pltpu.PrefetchScalarGridSpec(
            num_scalar_prefetch=2, grid=(steps + 1,),
            in_specs=[pl.BlockSpec((tm * ROW_CHUNKS, LANES), lambda i, pos, pend: (jnp.minimum(i, steps - 1), 0)),
                      pl.BlockSpec(h_sample.shape, lambda i, pos, pend: (0, 0))],
            out_specs=pl.BlockSpec(memory_space=pl.ANY),
            scratch_shapes=[pltpu.VMEM((MOE_ROWS * ROW_CHUNKS, LANES), f32), pltpu.SemaphoreType.DMA(()),
                            pltpu.SemaphoreType.DMA(())]),
        out_shape=jax.ShapeDtypeStruct((n_blocks * MOE_ROWS * ROW_CHUNKS, LANES), f32),
        compiler_params=_cparams(("arbitrary",)),
        name="moe_dispatch",
    )(pos, pend, h_prompt, h_sample)


def _expert_kernel(be_ref, nu_ref, xs_ref, wgu_ref, bgu_ref, wd_ref, bd_ref, y_ref, wgu_bf, wd_bf):
    b = pl.program_id(0)

    @pl.when(b < nu_ref[0])
    def _():
        @pl.when((b == 0) | (be_ref[b] != be_ref[jnp.maximum(b - 1, 0)]))
        def _():
            wgu_bf[...] = wgu_ref[...].astype(bf16)
            wd_bf[...] = wd_ref[...].astype(bf16)

        ff = wd_bf.shape[0]
        x = _load_row_tiles(xs_ref, MOE_ROWS)
        gu = jnp.dot(x.astype(bf16), wgu_bf[...], preferred_element_type=f32) + bgu_ref[...]
        gate = jnp.minimum(gu[:, 0:ff], SWIGLU_LIMIT)
        up = jnp.clip(gu[:, ff:2 * ff], -SWIGLU_LIMIT, SWIGLU_LIMIT)
        act = gate * jax.nn.sigmoid(SWIGLU_ALPHA * gate) * (up + 1.0)
        _store_row_tiles(y_ref, 0, jnp.dot(act.astype(bf16), wd_bf[...], preferred_element_type=f32) + bd_ref[...])

    @pl.when(b >= nu_ref[0])
    def _():
        y_ref[...] = jnp.zeros_like(y_ref)


def _experts(block_e, n_used, xs, w_gate_up, b_gate_up, w_down, b_down):
    n_exp, d, ff2 = w_gate_up.shape
    ff = ff2 // 2
    block = (MOE_ROWS * ROW_CHUNKS, LANES)
    rows = lambda b, be, nu: (b, 0)
    per_e = lambda b, be, nu: (be[b], 0, 0)
    return pl.pallas_call(
        _expert_kernel,
        grid_spec=pltpu.PrefetchScalarGridSpec(
            num_scalar_prefetch=2, grid=(xs.shape[0] // block[0],),
            in_specs=[pl.BlockSpec(block, rows),
                      pl.BlockSpec((None, d, ff2), per_e), pl.BlockSpec((None, 1, ff2), per_e),
                      pl.BlockSpec((None, ff, d), per_e), pl.BlockSpec((None, 1, d), per_e)],
            out_specs=pl.BlockSpec(block, rows),
            scratch_shapes=[pltpu.VMEM((d, ff2), bf16), pltpu.VMEM((ff, d), bf16)]),
        out_shape=jax.ShapeDtypeStruct(xs.shape, f32),
        compiler_params=_cparams(("arbitrary",)),
        name="moe_experts",
    )(block_e, n_used, xs, w_gate_up, b_gate_up.reshape(n_exp, 1, ff2), w_down, b_down.reshape(n_exp, 1, d))


def _combine_kernel(pos_ref, y_ref, x1_ref, comb_ref, g2_ref, lng_ref, lnb_ref, o_ref, buf, sem, *, alpha, n_steps, step_fn):
    tc = x1_ref.shape[0]
    step = step_fn()

    def rows(s, slot, wait):
        base = s * tc

        def body(r, c):
            for k in range(TOP_K):
                p = 0 if wait else pos_ref[(base + r) * TOP_K + k]
                cp = pltpu.make_async_copy(_row_tile(y_ref, p), _row_tile(buf, 0 if wait else r, slot * TOP_K + k),
                                           sem.at[slot])
                if wait:
                    cp.wait()
                else:
                    cp.start()
            return c

        lax.fori_loop(0, tc, body, 0)

    @pl.when(step == 0)
    def _():
        rows(0, 0, False)

    for sl in range(2):
        @pl.when((step + 1 < n_steps) & ((step + 1) % 2 == sl))
        def _(sl=sl):
            rows(step + 1, sl, False)

    slot = step % 2
    for sl in range(2):
        @pl.when(slot == sl)
        def _(sl=sl):
            rows(step, sl, True)

    comb = comb_ref[...]
    ffn = comb[:, 0:1] * _load_row_tiles(buf, tc, slot * TOP_K)
    for k in range(1, TOP_K):
        ffn = ffn + comb[:, k:k + 1] * _load_row_tiles(buf, tc, slot * TOP_K + k)
    o_ref[...] = _layer_norm(alpha * x1_ref[...] + g2_ref[...] * ffn, lng_ref[...], lnb_ref[...])


def _combine(pos, y, x1, comb, g2, lng, lnb, alpha, per_row_mods):
    d = x1.shape[-1]
    hbm = pl.BlockSpec(memory_space=pl.ANY)
    if per_row_mods:
        n = x1.shape[0]
        tc = min(COMBINE_TILE, n)
        grid = (n // tc,)
        row = lambda w: pl.BlockSpec((tc, w), lambda i, pos: (i, 0))
        mod = row(d)
        const = lambda shp: pl.BlockSpec(shp, lambda i, pos: (0,) * len(shp))
        step_fn = lambda: pl.program_id(0)
        n_steps = n // tc
        sem = ("arbitrary",)
        out_shape = jax.ShapeDtypeStruct((n, d), f32)
    else:
        b, s, _ = x1.shape
        tc = min(COMBINE_TILE, s)
        grid = (b, s // tc)
        row = lambda w: pl.BlockSpec((None, tc, w), lambda bi, i, pos: (bi, i, 0))
        mod = pl.BlockSpec((None, 1, d), lambda bi, i, pos: (bi, 0, 0))
        const = lambda shp: pl.BlockSpec(shp, lambda bi, i, pos: (0,) * len(shp))
        per_b = s // tc
        step_fn = lambda: pl.program_id(0) * per_b + pl.program_id(1)
        n_steps = b * per_b
        sem = ("arbitrary", "arbitrary")
        out_shape = jax.ShapeDtypeStruct((b, s, d), f32)
    return pl.pallas_call(
        functools.partial(_combine_kernel, alpha=alpha, n_steps=n_steps, step_fn=step_fn),
        grid_spec=pltpu.PrefetchScalarGridSpec(
            num_scalar_prefetch=1, grid=grid,
            in_specs=[hbm, row(d), row(LANES), mod, const((1, d)), const((1, d))],
            out_specs=row(d),
            scratch_shapes=[pltpu.VMEM((2 * TOP_K, tc * ROW_CHUNKS, LANES), f32), pltpu.SemaphoreType.DMA((2,))]),
        out_shape=out_shape,
        compiler_params=_cparams(sem),
        name="moe_combine_rows" if per_row_mods else "moe_combine_prompt",
    )(pos, y, x1, comb, g2, lng, lnb)


def _rope_tables(pos):
    n = pos.shape[0]
    inv_freq = ROPE_THETA ** (-jnp.arange(ROPE_HALF, dtype=f32) * 2.0 / ROPE_DIM)
    ang = pos[:, None] * inv_freq[None, :]
    cos, sin = jnp.cos(ang), jnp.sin(ang)
    rest = HEAD_DIM - ROPE_DIM
    z8, zr = jnp.zeros((n, ROPE_HALF), f32), jnp.zeros((n, rest), f32)
    c = jnp.concatenate([cos, cos, jnp.ones((n, rest), f32)], axis=1)
    s1 = jnp.concatenate([-sin, z8, zr], axis=1)
    s2 = jnp.concatenate([z8, sin, zr], axis=1)
    reps = LANES // HEAD_DIM
    return tuple(jnp.tile(a, (1, reps)) for a in (c, s1, s2))


def _rope_tables_t(pos):
    inv_freq = ROPE_THETA ** (-jnp.arange(ROPE_HALF, dtype=f32) * 2.0 / ROPE_DIM)
    ang = pos[:, None] * inv_freq[None, :]
    return jnp.cos(ang).T, jnp.sin(ang).T


def _pad_cols(a, width, value=0.0):
    return jnp.pad(a, ((0, 0), (0, width - a.shape[1])), constant_values=value)


def _layer(xp, xs, c_prompt, c_sample, cache_k, cache_v, cache_ki, state_conv, page_table, p, alpha):
    b, s, d = xp.shape
    db, t, _ = xs.shape
    ns = db * t
    assert t >= CONV_K - 1 and ns % SUBLANES == 0 and t <= SUBLANES

    n_c = b + db
    c_all = jnp.pad(jnp.concatenate([c_prompt, c_sample], axis=0), ((0, -n_c % SUBLANES), (0, 0)))
    ada = _ada(c_all, p["w_ada"], p["b_ada"])
    mods = [ada[:, j * d:(j + 1) * d] for j in range(6)]
    mp = [m[:b].reshape(b, 1, d) for m in mods]
    ms = [jnp.repeat(m[b:b + db], t, axis=0) for m in mods]

    sizes = (ATT_WIDTH, KV_WIDTH, KV_WIDTH, IDX_HEADS * IDX_DIM, IDX_DIM, IDX_HEADS, d // 2, d // 2, d // 2, d, d)
    cuts = [int(v) for v in np.cumsum(sizes)[:-1]]
    wq, wk, wv, wqi, wki, wwi, wxin, wbg, wcg, wga, wgb = jnp.split(p["w_in"], cuts, axis=1)
    w_proj = jnp.concatenate([wq, wk, wv, wqi, _pad_cols(wki, LANES), _pad_cols(wwi, LANES), wxin, wbg, wcg],
                             axis=1).astype(bf16)
    wg = jnp.concatenate([wga, wgb], axis=1).astype(bf16)
    post_w = (wg, p["w_branch_a"].astype(bf16), p["w_branch_b"].astype(bf16), p["w_o"].astype(bf16),
              _pad_cols(p["w_router"], LANES).astype(bf16), _pad_cols(p["b_router"][None, :], LANES, NEG_BIG),
              p["ln1_g"][None, :], p["ln1_b"][None, :])
    ln2 = (p["ln2_g"][None, :], p["ln2_b"][None, :])

    positions = jnp.arange(s, dtype=f32)
    w_t = jnp.concatenate([wk, wv, wki], axis=1).T.astype(bf16)
    q, qi, wi, kt, vt, kit, kt_b, vt_b, kit_b, conv_p, cst_p = _inproj_prompt(
        xp, mp[1], mp[0], _rope_tables(positions), _rope_tables_t(positions), w_proj, w_t, p["conv_w"])
    att_p = _attn_prompt(qi, wi, q, kit_b, kt_b, vt_b)

    past = page_table.shape[1] * PAGE_SIZE
    tabs_s = _rope_tables(jnp.tile(past + jnp.arange(t, dtype=f32), db))
    zeros = jnp.zeros((db, t, d // 2), f32)
    p1 = zeros.at[:, 0].set(state_conv[:, 1]).reshape(ns, d // 2)
    p2 = zeros.at[:, 0].set(state_conv[:, 0]).at[:, 1].set(state_conv[:, 1]).reshape(ns, d // 2)
    qs, ks, vs, qis, kis, wis, conv_s, u_s = _inproj_sample(xs.reshape(ns, d), ms[1], ms[0], tabs_s, w_proj,
                                                            p["conv_w"], p1, p2, t)
    pad_t = lambda a, rows: jnp.pad(a.reshape(db, t, -1), ((0, 0), (0, rows - t), (0, 0)))
    head_rows = lambda a: pad_t(a, SUBLANES).reshape(db, SUBLANES, -1, HEAD_DIM).transpose(0, 2, 1, 3).reshape(
        db, -1, HEAD_DIM)
    new_cols = lambda a: jnp.pad(a.reshape(db, t, -1).transpose(0, 2, 1), ((0, 0), (0, 0), (0, LANES - t)))
    n_pool = cache_k.shape[0]
    feature_major = lambda c: c.transpose(0, 2, 3, 1).reshape(n_pool, KV_WIDTH, PAGE_SIZE)
    bias_s = _select_sample(page_table, head_rows(qis), pad_t(wis, SUBLANES), new_cols(kis),
                            cache_ki.transpose(0, 2, 1), t)
    att_s = _attn_sample(page_table, head_rows(qs), bias_s, new_cols(ks), new_cols(vs),
                         feature_major(cache_k), feature_major(cache_v))
    att_s = att_s.reshape(db, N_HEADS, SUBLANES, HEAD_DIM)[:, :, :t].transpose(0, 2, 1, 3).reshape(ns, ATT_WIDTH)

    cnt0 = jnp.zeros((1, LANES), f32)
    x1p, h2p, tope_p, comb_p, rank_p, cnt1 = _post(xp, att_p, conv_p, (mp[1], mp[0], mp[2], mp[4], mp[3]),
                                                   post_w, cnt0, alpha, False)
    x1s, h2s, tope_s, comb_s, rank_s, cnt2 = _post(xs.reshape(ns, d), att_s, conv_s, (ms[1], ms[0], ms[2], ms[4], ms[3]),
                                                   post_w, cnt1, alpha, True)

    n_p = b * s
    counts = cnt2[0, :N_EXPERTS].astype(i32)
    padded = (counts + MOE_ROWS - 1) // MOE_ROWS * MOE_ROWS
    pend = jnp.cumsum(padded).astype(i32)
    pstart = pend - padded
    tope = jnp.concatenate([tope_p.reshape(n_p, LANES)[:, :TOP_K], tope_s[:, :TOP_K]], axis=0)
    rank = jnp.concatenate([rank_p.reshape(n_p, LANES)[:, :TOP_K], rank_s[:, :TOP_K]], axis=0)
    pos = (pstart[tope] + rank).astype(i32)
    n_blocks = ((n_p + ns) * TOP_K + N_EXPERTS * (MOE_ROWS - 1)) // MOE_ROWS
    n_used = pend[-1:] // MOE_ROWS
    blk = jnp.minimum(jnp.arange(n_blocks, dtype=i32), n_used[0] - 1)
    block_e = jnp.minimum(jnp.sum(pend[None, :] <= (blk * MOE_ROWS)[:, None], axis=1), N_EXPERTS - 1).astype(i32)
    pos_p, pos_s = pos[:n_p].reshape(-1), pos[n_p:].reshape(-1)

    rows = _dispatch(pos.reshape(-1), pend, h2p.reshape(n_p * ROW_CHUNKS, LANES), h2s, n_blocks)
    y = _experts(block_e, n_used, rows, p["w_gate_up"], p["b_gate_up"], p["w_down"], p["b_down"])
    yp = _combine(pos_p, y, x1p, comb_p, mp[5], *ln2, alpha, False)
    ys = _combine(pos_s, y, x1s, comb_s, ms[5], *ln2, alpha, True).reshape(db, t, d)

    token_major = lambda a: a.reshape(b, N_KV_HEADS, HEAD_DIM, s).transpose(0, 3, 1, 2)
    new_p = (token_major(kt), token_major(vt), kit.transpose(0, 2, 1), cst_p)
    new_s = (ks.reshape(db, t, N_KV_HEADS, HEAD_DIM), vs.reshape(db, t, N_KV_HEADS, HEAD_DIM),
             kis.reshape(db, t, IDX_DIM), u_s.reshape(db, t, d // 2)[:, t - (CONV_K - 1):])
    return yp, ys, new_p, new_s


def kernel(x_prompt, x_sample, cache_k, cache_v, cache_idx_k, state_conv, page_table, c_prompt, c_sample,
           w_ada, b_ada, w_in, conv_w, w_branch_a, w_branch_b, w_o, ln1_g, ln1_b, ln2_g, ln2_b,
           w_router, b_router, w_gate_up, b_gate_up, w_down, b_down):
    depth = w_ada.shape[0]
    alpha = float((2 * depth) ** 0.25)
    xp, xs = x_prompt, x_sample
    acc_p, acc_s = [], []
    for l in range(depth):
        p = {"w_ada": w_ada[l], "b_ada": b_ada[l], "w_in": w_in[l], "conv_w": conv_w[l],
             "w_branch_a": w_branch_a[l], "w_branch_b": w_branch_b[l], "w_o": w_o[l],
             "ln1_g": ln1_g[l], "ln1_b": ln1_b[l], "ln2_g": ln2_g[l], "ln2_b": ln2_b[l],
             "w_router": w_router[l], "b_router": b_router[l], "w_gate_up": w_gate_up[l],
             "b_gate_up": b_gate_up[l], "w_down": w_down[l], "b_down": b_down[l]}
        xp, xs, new_p, new_s = _layer(xp, xs, c_prompt, c_sample, cache_k[l], cache_v[l], cache_idx_k[l],
                                      state_conv[l], page_table, p, alpha)
        acc_p.append(new_p)
        acc_s.append(new_s)
    stack = lambda acc, j: jnp.stack([a[j] for a in acc])
    return (xp, xs, stack(acc_p, 0), stack(acc_p, 1), stack(acc_p, 2), stack(acc_p, 3),
            stack(acc_s, 0), stack(acc_s, 1), stack(acc_s, 2), stack(acc_s, 3))
```

```python
import functools

import jax
import jax.numpy as jnp
import numpy as np
from jax import lax
from jax.experimental import pallas as pl
from jax.experimental.pallas import tpu as pltpu

f32 = jnp.float32
bf16 = jnp.bfloat16
i32 = jnp.int32

N_HEADS = 8
HEAD_DIM = 64
N_KV_HEADS = 4
KV_GROUPS = N_HEADS // N_KV_HEADS
ATT_WIDTH = N_HEADS * HEAD_DIM
KV_WIDTH = N_KV_HEADS * HEAD_DIM
ROPE_DIM = HEAD_DIM // 4
ROPE_HALF = ROPE_DIM // 2
ROPE_THETA = 500000.0
IDX_HEADS = 8
IDX_DIM = 64
IDX_W_SCALE = (IDX_HEADS * IDX_DIM) ** -0.5
TOPK_MAX = 256
PAGE_SIZE = 128
CONV_K = 3
N_EXPERTS = 32
TOP_K = 4
SWIGLU_LIMIT = 7.0
SWIGLU_ALPHA = 1.702
LN_EPS = 1e-5

LANES = 128
SUBLANES = 8
ROW_CHUNKS = SUBLANES
VMEM_LIMIT = 56 * 1024 * 1024

ROW_TILE = 512
POST_TILE = 512
MOE_ROWS = 512
COMBINE_TILE = 256
Q_BLOCK = 128
KEY_CLASS = 1024
QK_SCALE = HEAD_DIM ** -0.5
SCORE_CHUNK = 512
SEQ_GROUP = 8
HALF_ROWS = 2 * SUBLANES
BISECT_UNROLL = 4
DMA_UNROLL = 4
i16 = jnp.int16
INT_MIN = int(np.iinfo(np.int32).min)
NEG_BIG = -1e30


def _cparams(sem):
    return pltpu.CompilerParams(dimension_semantics=sem, vmem_limit_bytes=VMEM_LIMIT)


def _mm(a, b):
    return jnp.dot(a.astype(bf16), b.astype(bf16), preferred_element_type=f32)


def _mm_nt(a, b):
    return lax.dot_general(a.astype(bf16), b.astype(bf16), (((1,), (1,)), ((), ())), preferred_element_type=f32)


def _store_row_tiles(ref, r0, value):
    n = value.shape[0]
    for j in range(ROW_CHUNKS):
        ref[pl.ds(r0 * ROW_CHUNKS + j, n, stride=ROW_CHUNKS), :] = value[:, j * LANES:(j + 1) * LANES]


def _load_row_tiles(ref, n, *lead):
    return jnp.concatenate([ref[lead + (pl.ds(j, n, stride=ROW_CHUNKS), slice(None))] for j in range(ROW_CHUNKS)],
                           axis=1)


def _row_tile(ref, r, *lead):
    start = r * ROW_CHUNKS if isinstance(r, int) else pl.multiple_of(r * ROW_CHUNKS, ROW_CHUNKS)
    return ref.at[lead + (pl.ds(start, ROW_CHUNKS), slice(None))]


def _layer_norm(y, g, b):
    mu = jnp.mean(y, axis=-1, keepdims=True)
    d = y - mu
    var = jnp.mean(d * d, axis=-1, keepdims=True)
    return d * lax.rsqrt(var + LN_EPS) * g + b


def _ada_kernel(c_ref, w_ref, b_ref, o_ref):
    o_ref[...] = _mm(c_ref[...], w_ref[...]) + b_ref[...]


def _ada(c_all, w_ada, b_ada):
    n, d = c_all.shape
    width = w_ada.shape[1]
    return pl.pallas_call(
        _ada_kernel,
        grid=(width // d,),
        in_specs=[pl.BlockSpec((n, d), lambda j: (0, 0)),
                  pl.BlockSpec((d, d), lambda j: (0, j)),
                  pl.BlockSpec((1, d), lambda j: (0, j))],
        out_specs=pl.BlockSpec((n, d), lambda j: (0, j)),
        out_shape=jax.ShapeDtypeStruct((n, width), f32),
        compiler_params=_cparams(("arbitrary",)),
        name="ada",
    )(c_all, w_ada, b_ada.reshape(1, width))


_SEG = {}
_off = 0
for _name, _w in (("q", ATT_WIDTH), ("k", KV_WIDTH), ("v", KV_WIDTH), ("qi", IDX_HEADS * IDX_DIM),
                  ("ki", LANES), ("wi", LANES), ("xin", 512), ("bg", 512), ("cg", 512)):
    _SEG[_name] = (_off, _off + _w)
    _off += _w
PROJ_WIDTH = _off


def _rope(t, c, s1, s2):
    w = t.shape[1]
    reps = w // LANES
    if reps > 1:
        c, s1, s2 = (jnp.tile(a, (1, reps)) for a in (c, s1, s2))
    return t * c + pltpu.roll(t, w - ROPE_HALF, 1) * s1 + pltpu.roll(t, ROPE_HALF, 1) * s2


def _project(x_ref, sc_ref, sh_ref, cos_ref, s1_ref, s2_ref, w_ref):
    h = x_ref[...] * (1.0 + sc_ref[...]) + sh_ref[...]
    hb = h.astype(bf16)
    c, s1, s2 = cos_ref[...], s1_ref[...], s2_ref[...]

    def seg(name):
        a, b = _SEG[name]
        return jnp.dot(hb, w_ref[:, a:b], preferred_element_type=f32)

    q = _rope(seg("q"), c, s1, s2)
    k = _rope(seg("k"), c, s1, s2)
    v = seg("v")
    qi = _rope(seg("qi"), c, s1, s2)
    ki = _rope(seg("ki"), c, s1, s2)[:, :IDX_DIM]
    wi = seg("wi")
    u = seg("cg") * seg("xin")
    return q, k, v, qi, ki, wi, u, seg("bg")


def _conv_out(u, bg, cw_ref, sh1, sh2):
    y = cw_ref[0:1, :] * sh2 + cw_ref[1:2, :] * sh1
    y = y + cw_ref[2:3, :] * u
    return bg * y


def _rope_rows(t, cos_t, sin_t, n_heads):
    parts = []
    for g in range(n_heads):
        base = g * HEAD_DIM
        r0, r1 = t[base:base + ROPE_HALF], t[base + ROPE_HALF:base + ROPE_DIM]
        parts += [r0 * cos_t - r1 * sin_t, r1 * cos_t + r0 * sin_t, t[base + ROPE_DIM:base + HEAD_DIM]]
    return jnp.concatenate(parts, axis=0)


def _inproj_prompt_kernel(x_ref, sc_ref, sh_ref, cos_ref, s1_ref, s2_ref, cost_ref, sint_ref, w_ref, wt_ref, cw_ref,
                          q_ref, qi_ref, wi_ref, kt_ref, vt_ref, kit_ref, ktb_ref, vtb_ref, kitb_ref,
                          conv_ref, cst_ref, carry_ref):
    @pl.when(pl.program_id(1) == 0)
    def _():
        carry_ref[...] = jnp.zeros_like(carry_ref)

    hb = (x_ref[...] * (1.0 + sc_ref[...]) + sh_ref[...]).astype(bf16)
    c, s1, s2 = cos_ref[...], s1_ref[...], s2_ref[...]

    def seg(name):
        a, b = _SEG[name]
        return jnp.dot(hb, w_ref[:, a:b], preferred_element_type=f32)

    q = _rope(seg("q"), c, s1, s2) * QK_SCALE
    qi = _rope(seg("qi"), c, s1, s2)
    for h in range(N_HEADS):
        q_ref[h] = q[:, h * HEAD_DIM:(h + 1) * HEAD_DIM].astype(bf16)
    for h in range(IDX_HEADS):
        qi_ref[h] = qi[:, h * IDX_DIM:(h + 1) * IDX_DIM].astype(bf16)
    wi_ref[...] = seg("wi") * IDX_W_SCALE

    t = lax.dot_general(wt_ref[...], hb, (((1,), (1,)), ((), ())), preferred_element_type=f32)
    cos_t, sin_t = cost_ref[...], sint_ref[...]
    kt = _rope_rows(t[0:KV_WIDTH], cos_t, sin_t, N_KV_HEADS)
    vt = t[KV_WIDTH:2 * KV_WIDTH]
    kit = _rope_rows(t[2 * KV_WIDTH:2 * KV_WIDTH + IDX_DIM], cos_t, sin_t, 1)
    kt_ref[...], vt_ref[...], kit_ref[...] = kt, vt, kit
    ktb_ref[...], vtb_ref[...], kitb_ref[...] = kt.astype(bf16), vt.astype(bf16), kit.astype(bf16)

    u = seg("cg") * seg("xin")
    bg = seg("bg")
    tm = u.shape[0]
    rows = lax.broadcasted_iota(i32, u.shape, 0)
    c0, c1 = carry_ref[0:1, :], carry_ref[1:2, :]
    sh1 = jnp.where(rows == 0, c1, pltpu.roll(u, 1, 0))
    sh2 = jnp.where(rows == 0, c0, jnp.where(rows == 1, c1, pltpu.roll(u, 2, 0)))
    conv_ref[...] = _conv_out(u, bg, cw_ref, sh1, sh2)
    tail = u[tm - 2:tm, :]
    carry_ref[0:2, :] = tail
    cst_ref[...] = tail


def _inproj_sample_kernel(x_ref, sc_ref, sh_ref, cos_ref, s1_ref, s2_ref, w_ref, cw_ref, p1_ref, p2_ref,
                          q_ref, k_ref, v_ref, qi_ref, ki_ref, wi_ref, conv_ref, u_ref, *, dec_seq):
    q, k, v, qi, ki, wi, u, bg = _project(x_ref, sc_ref, sh_ref, cos_ref, s1_ref, s2_ref, w_ref)
    q_ref[...], k_ref[...], v_ref[...], qi_ref[...], ki_ref[...], wi_ref[...] = q, k, v, qi, ki, wi
    t = lax.broadcasted_iota(i32, u.shape, 0) % dec_seq
    sh1 = jnp.where(t == 0, p1_ref[...], pltpu.roll(u, 1, 0))
    sh2 = jnp.where(t < 2, p2_ref[...], pltpu.roll(u, 2, 0))
    conv_ref[...] = _conv_out(u, bg, cw_ref, sh1, sh2)
    u_ref[...] = u


def _inproj_prompt(x, sc1, sh1, tabs, tabs_t, w_proj, w_t, conv_w):
    b, s, d = x.shape
    tm = min(ROW_TILE, s)
    row = lambda w: pl.BlockSpec((None, tm, w), lambda bi, i: (bi, i, 0))
    heads = lambda n: pl.BlockSpec((None, n, tm, HEAD_DIM), lambda bi, i: (bi, 0, i, 0))
    col = lambda r: pl.BlockSpec((None, r, tm), lambda bi, i: (bi, 0, i))
    per_b = pl.BlockSpec((None, 1, d), lambda bi, i: (bi, 0, 0))
    tab = pl.BlockSpec((tm, LANES), lambda bi, i: (i, 0))
    tab_t = pl.BlockSpec((ROPE_HALF, tm), lambda bi, i: (0, i))
    const = lambda a: pl.BlockSpec(a.shape, lambda bi, i: (0,) * a.ndim)
    feat = (KV_WIDTH, KV_WIDTH, IDX_DIM)
    return pl.pallas_call(
        _inproj_prompt_kernel,
        grid=(b, s // tm),
        in_specs=[row(d), per_b, per_b, tab, tab, tab, tab_t, tab_t, const(w_proj), const(w_t), const(conv_w)],
        out_specs=[heads(N_HEADS), heads(IDX_HEADS), row(LANES)] + [col(r) for r in feat] + [col(r) for r in feat]
        + [row(512), pl.BlockSpec((None, CONV_K - 1, 512), lambda bi, i: (bi, 0, 0))],
        out_shape=[jax.ShapeDtypeStruct((b, N_HEADS, s, HEAD_DIM), bf16),
                   jax.ShapeDtypeStruct((b, IDX_HEADS, s, IDX_DIM), bf16),
                   jax.ShapeDtypeStruct((b, s, LANES), f32)]
        + [jax.ShapeDtypeStruct((b, r, s), f32) for r in feat]
        + [jax.ShapeDtypeStruct((b, r, s), bf16) for r in feat]
        + [jax.ShapeDtypeStruct((b, s, 512), f32), jax.ShapeDtypeStruct((b, CONV_K - 1, 512), f32)],
        scratch_shapes=[pltpu.VMEM((SUBLANES, 512), f32)],
        compiler_params=_cparams(("arbitrary", "arbitrary")),
        name="inproj_prompt",
    )(x, sc1, sh1, *tabs, *tabs_t, w_proj, w_t, conv_w)


def _inproj_sample(x, sc1, sh1, tabs, w_proj, conv_w, p1, p2, dec_seq):
    n, d = x.shape
    widths = (ATT_WIDTH, KV_WIDTH, KV_WIDTH, IDX_HEADS * IDX_DIM, IDX_DIM, LANES, 512, 512)
    full = lambda r, w: pl.BlockSpec((r, w), lambda i: (0, 0))
    return pl.pallas_call(
        functools.partial(_inproj_sample_kernel, dec_seq=dec_seq),
        grid=(1,),
        in_specs=[full(n, d), full(n, d), full(n, d), full(n, LANES), full(n, LANES), full(n, LANES),
                  full(d, PROJ_WIDTH), full(CONV_K, 512), full(n, 512), full(n, 512)],
        out_specs=[full(n, w) for w in widths],
        out_shape=[jax.ShapeDtypeStruct((n, w), f32) for w in widths],
        compiler_params=_cparams(("arbitrary",)),
        name="inproj_sample",
    )(x, sc1, sh1, *tabs, w_proj, conv_w, p1, p2)


def _sort_key(score):
    bits = pltpu.bitcast(jnp.where(score == 0.0, 0.0, score), i32)
    return bits ^ ((bits >> 31) & jnp.int32(0x7FFFFFFF))


def _bisect16(ref, n_cols, need, floor_counts):
    n_groups = ref.shape[0] // HALF_ROWS
    one, zero = jnp.ones((), i16), jnp.zeros((), i16)

    def refine(bit, state):
        out = []
        for g in range(n_groups):
            rows = slice(g * HALF_ROWS, (g + 1) * HALF_ROWS)
            t, n = state[g]
            cand = t + jnp.left_shift(jnp.int32(1), bit)
            cand16 = cand.astype(i16)
            acc = jnp.zeros((HALF_ROWS, LANES), i16)
            for c in range(n_cols // LANES):
                acc = acc + jnp.where(ref[rows, c * LANES:(c + 1) * LANES] >= cand16, one, zero)
            cnt = jnp.sum(acc.astype(f32), axis=1, keepdims=True)
            ok = cnt >= need[rows]
            out.append((jnp.where(ok, cand, t), jnp.where(ok, cnt, n)))
        return tuple(out)

    def outer(i, state):
        for j in range(BISECT_UNROLL):
            state = refine(15 - (i * BISECT_UNROLL + j), state)
        return state

    state = tuple((jnp.full((HALF_ROWS, 1), -32768, i32), floor_counts[g * HALF_ROWS:(g + 1) * HALF_ROWS])
                  for g in range(n_groups))
    state = lax.fori_loop(0, 16 // BISECT_UNROLL, outer, state)
    return jnp.concatenate([s[0] for s in state], axis=0), jnp.concatenate([s[1] for s in state], axis=0)


def _select_bias(key_ref, hi_ref, lo_ref, n_cols, topk):
    n_rows = key_ref.shape[0]
    kf = float(topk)
    one, zero = jnp.ones((), i16), jnp.zeros((), i16)
    for c0 in range(0, n_cols, SCORE_CHUNK):
        cols = slice(c0, min(c0 + SCORE_CHUNK, n_cols))
        k = key_ref[:, cols]
        hi_ref[:, cols] = (k >> 16).astype(i16)
        lo_ref[:, cols] = ((k & 0xFFFF) - 32768).astype(i16)
    everything = jnp.full((n_rows, 1), float(n_cols), f32)
    t_hi, _ = _bisect16(hi_ref, n_cols, jnp.full((n_rows, 1), kf, f32), everything)
    t_hi16 = t_hi.astype(i16)
    acc_gt = jnp.zeros((n_rows, LANES), i16)
    acc_eq = jnp.zeros((n_rows, LANES), i16)
    for c in range(n_cols // LANES):
        cols = slice(c * LANES, (c + 1) * LANES)
        h = hi_ref[:, cols]
        same = h == t_hi16
        acc_gt = acc_gt + jnp.where(h > t_hi16, one, zero)
        acc_eq = acc_eq + jnp.where(same, one, zero)
        lo_ref[:, cols] = jnp.where(same, lo_ref[:, cols], jnp.full((), -32768, i16))
    above = jnp.sum(acc_gt.astype(f32), axis=1, keepdims=True)
    sharing = jnp.sum(acc_eq.astype(f32), axis=1, keepdims=True)
    t_lo, n_lo = _bisect16(lo_ref, n_cols, kf - above, sharing)
    t = jnp.left_shift(t_hi, 16) + (t_lo + 32768)
    n_ge = above + n_lo

    col = lax.broadcasted_iota(i32, (n_rows, n_cols), 1)
    n_bits = int(n_cols).bit_length()

    def tie_search(_):
        def count(pred):
            return jnp.sum(jnp.where(pred, 1.0, 0.0), axis=1, keepdims=True)

        need = kf - count(key_ref[:, 0:n_cols] > t)

        def tstep(i, j):
            cand = j + jnp.left_shift(jnp.int32(1), n_bits - 1 - i)
            below = count((key_ref[:, 0:n_cols] == t) & (col < cand))
            return jnp.where(below < need, cand, j)

        return lax.fori_loop(0, n_bits, tstep, jnp.zeros((n_rows, 1), i32))

    jmax = lax.cond(jnp.max(n_ge) > kf, tie_search, lambda _: jnp.full((n_rows, 1), n_cols, i32), 0)
    jmax = jnp.where(t == INT_MIN, n_cols, jmax)
    demoted = jnp.maximum(key_ref[:, 0:n_cols], INT_MIN + 1) - jnp.where(col > jmax, 1, 0)
    return jnp.where(demoted >= jnp.maximum(t, INT_MIN + 2), 0.0, -jnp.inf)


def _masked_attention(q, kt, vt, bias):
    logits = jnp.dot(q, kt, preferred_element_type=f32) + bias
    m = jnp.max(logits, axis=1, keepdims=True)
    p = jnp.exp(logits - m)
    den = jnp.sum(p, axis=1, keepdims=True)
    o = lax.dot_general(p.astype(bf16), vt, (((1,), (1,)), ((), ())), preferred_element_type=f32)
    return o / den


def _index_scores(qi_rows, wi, kit, n_heads_rows):
    dots = jnp.dot(qi_rows, kit, preferred_element_type=f32)
    r = n_heads_rows
    score = wi[:, 0:1] * jnp.maximum(dots[0:r], 0.0)
    for h in range(1, IDX_HEADS):
        score = score + wi[:, h:h + 1] * jnp.maximum(dots[h * r:(h + 1) * r], 0.0)
    return score


def _attn_prompt_body(qi_ref, wi_ref, q_ref, kit_ref, kt_ref, vt_ref, o_ref, key_ref, hi_ref, lo_ref,
                      s0, n_keys, topk):
    qi_rows = qi_ref[...].reshape(IDX_HEADS * Q_BLOCK, IDX_DIM)
    wi = wi_ref[...]
    chunk = min(SCORE_CHUNK, n_keys)
    for c0 in range(0, n_keys, chunk):
        score = _index_scores(qi_rows, wi, kit_ref[:, c0:c0 + chunk], Q_BLOCK)
        col = c0 + lax.broadcasted_iota(i32, (Q_BLOCK, chunk), 1)
        qpos = s0 + lax.broadcasted_iota(i32, (Q_BLOCK, chunk), 0)
        key_ref[:, c0:c0 + chunk] = jnp.where(col <= qpos, _sort_key(score), INT_MIN)
    bias = _select_bias(key_ref, hi_ref, lo_ref, n_keys, topk)
    for h in range(N_HEADS):
        g = h // KV_GROUPS
        o = _masked_attention(q_ref[h], kt_ref[g * HEAD_DIM:(g + 1) * HEAD_DIM, 0:n_keys],
                              vt_ref[g * HEAD_DIM:(g + 1) * HEAD_DIM, 0:n_keys], bias)
        o_ref[:, h * HEAD_DIM:(h + 1) * HEAD_DIM] = o


def _attn_prompt_kernel(qi_ref, wi_ref, q_ref, kit_ref, kt_ref, vt_ref, o_ref, key_ref, hi_ref, lo_ref,
                        *, key_class, topk):
    s0 = pl.program_id(1) * Q_BLOCK
    n_classes = kit_ref.shape[1] // key_class
    cls = (s0 + Q_BLOCK - 1) // key_class
    for c in range(n_classes):
        @pl.when(cls == c)
        def _(c=c):
            _attn_prompt_body(qi_ref, wi_ref, q_ref, kit_ref, kt_ref, vt_ref, o_ref, key_ref, hi_ref, lo_ref,
                              s0, (c + 1) * key_class, topk)


def _attn_prompt(qi, wi, q, kit, kt, vt):
    b, _, s, _ = q.shape
    key_class = min(KEY_CLASS, s)
    topk = min(TOPK_MAX, s // 4)
    blk = lambda w: pl.BlockSpec((None, Q_BLOCK, w), lambda bi, i: (bi, i, 0))
    heads = lambda n: pl.BlockSpec((None, n, Q_BLOCK, HEAD_DIM), lambda bi, i: (bi, 0, i, 0))
    whole = lambda r: pl.BlockSpec((None, r, s), lambda bi, i: (bi, 0, 0))
    return pl.pallas_call(
        functools.partial(_attn_prompt_kernel, key_class=key_class, topk=topk),
        grid=(b, s // Q_BLOCK),
        in_specs=[heads(IDX_HEADS), blk(LANES), heads(N_HEADS), whole(IDX_DIM), whole(KV_WIDTH), whole(KV_WIDTH)],
        out_specs=blk(ATT_WIDTH),
        out_shape=jax.ShapeDtypeStruct((b, s, ATT_WIDTH), f32),
        scratch_shapes=[pltpu.VMEM((Q_BLOCK, s), i32), pltpu.VMEM((Q_BLOCK, s), i16), pltpu.VMEM((Q_BLOCK, s), i16)],
        compiler_params=_cparams(("arbitrary", "arbitrary")),
        name="attn_prompt",
    )(qi, wi, q, kit, kt, vt)


def _select_sample_kernel(pt_ref, qi_ref, wi_ref, kin_ref, cki_ref, bias_ref, kit_buf, key_ref, hi_ref, lo_ref, sem,
                          *, n_pages, dec_seq, topk, group):
    step = pl.program_id(0)
    past = n_pages * PAGE_SIZE
    n_keys = past + LANES

    def copy(j):
        s, p = j // n_pages, j % n_pages
        page = pt_ref[(step * group + s) * n_pages + p]
        cols = pl.ds(pl.multiple_of(p * PAGE_SIZE, PAGE_SIZE), PAGE_SIZE)
        return pltpu.make_async_copy(cki_ref.at[page], kit_buf.at[s, :, cols], sem)

    lax.fori_loop(0, group * n_pages, lambda j, c: (copy(j).start(), c)[1], 0)
    kit_buf[:, :, past:n_keys] = kin_ref[...]
    lax.fori_loop(0, group * n_pages, lambda j, c: (copy(j).wait(), c)[1], 0)

    col = lax.broadcasted_iota(i32, (SUBLANES, n_keys), 1)
    tok = lax.broadcasted_iota(i32, (SUBLANES, n_keys), 0)
    allowed = (col < past) | ((col - past <= tok) & (col - past < dec_seq))
    for s in range(group):
        score = _index_scores(qi_ref[s].astype(bf16), wi_ref[s] * IDX_W_SCALE, kit_buf[s].astype(bf16), SUBLANES)
        key_ref[s * SUBLANES:(s + 1) * SUBLANES, :] = jnp.where(allowed, _sort_key(score), INT_MIN)
    bias = _select_bias(key_ref, hi_ref, lo_ref, n_keys, topk)
    for s in range(group):
        bias_ref[s] = bias[s * SUBLANES:(s + 1) * SUBLANES, :]


def _select_sample(page_table, qi_rows, wi8, kit_new, cache_kit, dec_seq):
    db, n_pages = page_table.shape
    n_keys = n_pages * PAGE_SIZE + LANES
    topk = min(TOPK_MAX, (n_pages * PAGE_SIZE + dec_seq) // 4)
    group = min(SEQ_GROUP, db)
    per = lambda r, w: pl.BlockSpec((group, r, w), lambda i, pt: (i, 0, 0))
    return pl.pallas_call(
        functools.partial(_select_sample_kernel, n_pages=n_pages, dec_seq=dec_seq, topk=topk, group=group),
        grid_spec=pltpu.PrefetchScalarGridSpec(
            num_scalar_prefetch=1, grid=(db // group,),
            in_specs=[per(IDX_HEADS * SUBLANES, IDX_DIM), per(SUBLANES, LANES), per(IDX_DIM, LANES),
                      pl.BlockSpec(memory_space=pl.ANY)],
            out_specs=per(SUBLANES, n_keys),
            scratch_shapes=[pltpu.VMEM((group, IDX_DIM, n_keys), f32), pltpu.VMEM((group * SUBLANES, n_keys), i32),
                            pltpu.VMEM((group * SUBLANES, n_keys), i16), pltpu.VMEM((group * SUBLANES, n_keys), i16),
                            pltpu.SemaphoreType.DMA(())]),
        out_shape=jax.ShapeDtypeStruct((db, SUBLANES, n_keys), f32),
        compiler_params=_cparams(("arbitrary",)),
        name="select_sample",
    )(page_table.reshape(-1), qi_rows, wi8, kit_new, cache_kit)


def _attn_sample_kernel(pt_ref, q_ref, bias_ref, ktn_ref, vtn_ref, ckt_ref, cvt_ref, o_ref, kt_buf, vt_buf, sem,
                        *, n_pages, n_seqs):
    b = pl.program_id(0)
    past = n_pages * PAGE_SIZE
    n_keys = past + LANES

    def pages(seq, slot, wait):
        def body(p, c):
            page = pt_ref[seq * n_pages + p]
            cols = pl.ds(pl.multiple_of(p * PAGE_SIZE, PAGE_SIZE), PAGE_SIZE)
            for src, dst, j in ((ckt_ref, kt_buf, 0), (cvt_ref, vt_buf, 1)):
                cp = pltpu.make_async_copy(src.at[page], dst.at[slot, :, cols], sem.at[j, slot])
                if wait:
                    cp.wait()
                else:
                    cp.start()
            return c

        lax.fori_loop(0, n_pages, body, 0)

    @pl.when(b == 0)
    def _():
        pages(0, 0, False)

    @pl.when(b + 1 < n_seqs)
    def _():
        pages(b + 1, (b + 1) % 2, False)

    slot = b % 2
    kt_buf[slot, :, past:n_keys] = ktn_ref[...]
    vt_buf[slot, :, past:n_keys] = vtn_ref[...]
    pages(b, slot, True)

    bias = bias_ref[...]
    bias2 = jnp.concatenate([bias] * KV_GROUPS, axis=0)
    rows = KV_GROUPS * SUBLANES
    for g in range(N_KV_HEADS):
        kt = kt_buf[slot, g * HEAD_DIM:(g + 1) * HEAD_DIM, :].astype(bf16)
        vt = vt_buf[slot, g * HEAD_DIM:(g + 1) * HEAD_DIM, :].astype(bf16)
        q = (q_ref[g * rows:(g + 1) * rows, :] * QK_SCALE).astype(bf16)
        o_ref[g * rows:(g + 1) * rows, :] = _masked_attention(q, kt, vt, bias2)


def _attn_sample(page_table, q_rows, bias, kt_new, vt_new, cache_kt, cache_vt):
    db, n_pages = page_table.shape
    n_keys = n_pages * PAGE_SIZE + LANES
    per = lambda r, w: pl.BlockSpec((None, r, w), lambda bi, pt: (bi, 0, 0))
    hbm = pl.BlockSpec(memory_space=pl.ANY)
    return pl.pallas_call(
        functools.partial(_attn_sample_kernel, n_pages=n_pages, n_seqs=db),
        grid_spec=pltpu.PrefetchScalarGridSpec(
            num_scalar_prefetch=1, grid=(db,),
            in_specs=[per(N_HEADS * SUBLANES, HEAD_DIM), per(SUBLANES, n_keys), per(KV_WIDTH, LANES),
                      per(KV_WIDTH, LANES), hbm, hbm],
            out_specs=per(N_HEADS * SUBLANES, HEAD_DIM),
            scratch_shapes=[pltpu.VMEM((2, KV_WIDTH, n_keys), f32), pltpu.VMEM((2, KV_WIDTH, n_keys), f32),
                            pltpu.SemaphoreType.DMA((2, 2))]),
        out_shape=jax.ShapeDtypeStruct((db, N_HEADS * SUBLANES, HEAD_DIM), f32),
        compiler_params=_cparams(("arbitrary",)),
        name="attn_sample",
    )(page_table.reshape(-1), q_rows, bias, kt_new, vt_new, cache_kt, cache_vt)


def _post_body(x_ref, att_ref, conv_ref, sc1_ref, sh1_ref, g1_ref, sc2_ref, sh2_ref,
               wg_ref, wa_ref, wb_ref, wo_ref, wr_ref, br_ref, lng_ref, lnb_ref, cnt_in_ref,
               x1_ref, h2_ref, tope_ref, comb_ref, rank_ref, cnt_ref, *, alpha):
    del cnt_in_ref
    n = x_ref.shape[0]
    sub = min(ROW_TILE, n)
    cnt = cnt_ref[...]
    for r0 in range(0, n, sub):
        cnt = _post_rows(slice(r0, r0 + sub), cnt, x_ref, att_ref, conv_ref, sc1_ref, sh1_ref, g1_ref, sc2_ref,
                         sh2_ref, wg_ref, wa_ref, wb_ref, wo_ref, wr_ref, br_ref, lng_ref, lnb_ref,
                         x1_ref, h2_ref, tope_ref, comb_ref, rank_ref, alpha)
    cnt_ref[...] = cnt


def _post_rows(rows, cnt, x_ref, att_ref, conv_ref, sc1_ref, sh1_ref, g1_ref, sc2_ref, sh2_ref,
               wg_ref, wa_ref, wb_ref, wo_ref, wr_ref, br_ref, lng_ref, lnb_ref,
               x1_ref, h2_ref, tope_ref, comb_ref, rank_ref, alpha):
    mod = lambda ref: ref[...] if ref.shape[0] == 1 else ref[rows, :]
    d = x_ref.shape[-1]
    x = x_ref[rows, :]
    hb = (x * (1.0 + mod(sc1_ref)) + mod(sh1_ref)).astype(bf16)
    ga = jnp.dot(hb, wg_ref[:, 0:d], preferred_element_type=f32)
    gb = jnp.dot(hb, wg_ref[:, d:2 * d], preferred_element_type=f32)
    a = _mm(att_ref[rows, :], wa_ref[...])
    c = _mm(conv_ref[rows, :], wb_ref[...])
    mix = _mm(jax.nn.sigmoid(ga) * a + jax.nn.sigmoid(gb) * c, wo_ref[...])
    x1 = _layer_norm(alpha * x + mod(g1_ref) * mix, lng_ref[...], lnb_ref[...])
    x1_ref[rows, :] = x1
    h2 = x1 * (1.0 + mod(sc2_ref)) + mod(sh2_ref)
    _store_row_tiles(h2_ref, rows.start, h2)
    logits =_mm(h2, wr_ref[...]) + br_ref[...]

    tm = logits.shape[0]
    lane = lax.broadcasted_iota(i32, (tm, LANES), 1)
    tope = jnp.zeros((tm, LANES), i32)
    topv = jnp.full((tm, LANES), -jnp.inf, f32)
    hot = jnp.zeros((tm, LANES), f32)
    picks = []
    work = logits
    for k in range(TOP_K):
        m = jnp.max(work, axis=1, keepdims=True)
        idx = jnp.min(jnp.where(work == m, lane, LANES), axis=1, keepdims=True)
        hit = lane == idx
        tope = jnp.where(lane == k, idx, tope)
        topv = jnp.where(lane == k, m, topv)
        hot = hot + jnp.where(hit, 1.0, 0.0)
        work = jnp.where(hit, -jnp.inf, work)
        picks.append(idx)
    e = jnp.where(lane < TOP_K, jnp.exp(topv - jnp.max(topv, axis=1, keepdims=True)), 0.0)
    comb_ref[rows, :] = e / jnp.sum(e, axis=1, keepdims=True)
    tope_ref[rows, :] = tope

    r_i = lax.broadcasted_iota(i32, (tm, tm), 0)
    c_i = lax.broadcasted_iota(i32, (tm, tm), 1)
    lower = jnp.where(c_i < r_i, 1.0, 0.0)
    before = _mm(lower, hot) + cnt
    rank = jnp.zeros((tm, LANES), f32)
    for k in range(TOP_K):
        rk = jnp.sum(jnp.where(lane == picks[k], before, 0.0), axis=1, keepdims=True)
        rank = jnp.where(lane == k, rk, rank)
    rank_ref[rows, :] = rank.astype(i32)
    return cnt + jnp.sum(hot, axis=0, keepdims=True)


def _post(x, att, conv, mods, weights, cnt_in, alpha, per_row_mods):
    wg, wa, wb, wo, wr, br, lng, lnb = weights
    d = x.shape[-1]
    if per_row_mods:
        n = x.shape[0]
        grid = (1,)
        row = lambda w: pl.BlockSpec((n, w), lambda i: (0, 0))
        row_tiles = pl.BlockSpec((n * ROW_CHUNKS, LANES), lambda i: (0, 0))
        mod = row(d)
        const = lambda shp: pl.BlockSpec(shp, lambda i: (0,) * len(shp))
        lead = (n,)
        first = lambda: pl.program_id(0) == 0
        sem = ("arbitrary",)
    else:
        b, s, _ = x.shape
        tm = min(POST_TILE, s)
        grid = (b, s // tm)
        row = lambda w: pl.BlockSpec((None, tm, w), lambda bi, i: (bi, i, 0))
        row_tiles = pl.BlockSpec((None, tm * ROW_CHUNKS, LANES), lambda bi, i: (bi, i, 0))
        mod = pl.BlockSpec((None, 1, d), lambda bi, i: (bi, 0, 0))
        const = lambda shp: pl.BlockSpec(shp, lambda bi, i: (0,) * len(shp))
        lead = (b, s)
        first = lambda: (pl.program_id(0) == 0) & (pl.program_id(1) == 0)
        sem = ("arbitrary", "arbitrary")

    def body(*refs):
        cnt_in_ref, cnt_ref = refs[16], refs[22]

        @pl.when(first())
        def _():
            cnt_ref[...] = cnt_in_ref[...]

        _post_body(*refs, alpha=alpha)

    assert d == ROW_CHUNKS * LANES
    tiles = lead[:-1] + (lead[-1] * ROW_CHUNKS, LANES)
    outs = [jax.ShapeDtypeStruct(lead + (d,), f32), jax.ShapeDtypeStruct(tiles, f32),
            jax.ShapeDtypeStruct(lead + (LANES,), i32), jax.ShapeDtypeStruct(lead + (LANES,), f32),
            jax.ShapeDtypeStruct(lead + (LANES,), i32), jax.ShapeDtypeStruct((1, LANES), f32)]
    return pl.pallas_call(
        body,
        grid=grid,
        in_specs=[row(d), row(ATT_WIDTH), row(512), mod, mod, mod, mod, mod,
                  const(wg.shape), const(wa.shape), const(wb.shape), const(wo.shape), const(wr.shape),
                  const(br.shape), const(lng.shape), const(lnb.shape), const((1, LANES))],
        out_specs=[row(d), row_tiles, row(LANES), row(LANES), row(LANES), const((1, LANES))],
        out_shape=outs,
        compiler_params=_cparams(sem),
        name="post_rows" if per_row_mods else "post_prompt",
    )(x, att, conv, *mods, wg, wa, wb, wo, wr, br, lng, lnb, cnt_in)


def _dispatch_kernel(pos_ref, pend_ref, hp_ref, hs_ref, xs_ref, zero_ref, sem, zsem, *, n_prompt_steps, n_blocks):
    tm = hp_ref.shape[0] // ROW_CHUNKS
    block_rows = MOE_ROWS * ROW_CHUNKS
    step = pl.program_id(0)

    @pl.when(step == 0)
    def _():
        zero_ref[...] = jnp.zeros_like(zero_ref)

        def zero_block(row0, wait):
            dst = xs_ref.at[pl.ds(pl.multiple_of(row0 * ROW_CHUNKS, block_rows), block_rows), :]
            cp = pltpu.make_async_copy(zero_ref, dst, zsem)
            if wait:
                cp.wait()
            else:
                cp.start()

        def fill(wait):
            def last_block(e, c):
                end = pend_ref[e]
                begin = jnp.where(e == 0, 0, pend_ref[jnp.maximum(e - 1, 0)])

                @pl.when(end > begin)
                def _():
                    zero_block(end - MOE_ROWS, wait)

                return c

            lax.fori_loop(0, N_EXPERTS, last_block, 0)
            n_used = pend_ref[N_EXPERTS - 1] // MOE_ROWS
            lax.fori_loop(n_used, n_blocks, lambda blk, c: (zero_block(blk * MOE_ROWS, wait), c)[1], 0)

        fill(False)
        fill(True)

    def scatter(src_ref, base):
        def issue(r2, c):
            for u in range(DMA_UNROLL):
                r = r2 * DMA_UNROLL + u
                for k in range(TOP_K):
                    p = pos_ref[(base + r) * TOP_K + k]
                    pltpu.make_async_copy(_row_tile(src_ref, r), _row_tile(xs_ref, p), sem).start()
            return c

        def drain(r, c):
            for k in range(TOP_K):
                pltpu.make_async_copy(_row_tile(src_ref, 0), _row_tile(xs_ref, 0), sem).wait()
            return c

        lax.fori_loop(0, src_ref.shape[0] // ROW_CHUNKS // DMA_UNROLL, issue, 0)
        lax.fori_loop(0, src_ref.shape[0] // ROW_CHUNKS, drain, 0)

    @pl.when(step < n_prompt_steps)
    def _():
        scatter(hp_ref, step * tm)

    @pl.when(step == n_prompt_steps)
    def _():
        scatter(hs_ref, n_prompt_steps * tm)


def _dispatch(pos, pend, h_prompt, h_sample, n_blocks):
    n = h_prompt.shape[0] // ROW_CHUNKS
    tm = min(ROW_TILE, n)
    steps = n // tm
    return pl.pallas_call(
        functools.partial(_dispatch_kernel, n_prompt_steps=steps, n_blocks=n_blocks),
        grid_spec=pltpu.PrefetchScalarGridSpec(
            num_scalar_prefetch=2, grid=(steps + 1,),
            in_specs=[pl.BlockSpec((tm * ROW_CHUNKS, LANES), lambda i, pos, pend: (jnp.minimum(i, steps - 1), 0)),
                      pl.BlockSpec(h_sample.shape, lambda i, pos, pend: (0, 0))],
            out_specs=pl.BlockSpec(memory_space=pl.ANY),
            scratch_shapes=[pltpu.VMEM((MOE_ROWS * ROW_CHUNKS, LANES), f32), pltpu.SemaphoreType.DMA(()),
                            pltpu.SemaphoreType.DMA(())]),
        out_shape=jax.ShapeDtypeStruct((n_blocks * MOE_ROWS * ROW_CHUNKS, LANES), f32),
        compiler_params=_cparams(("arbitrary",)),
        name="moe_dispatch",
    )(pos, pend, h_prompt, h_sample)


def _expert_kernel(be_ref, nu_ref, xs_ref, wgu_ref, bgu_ref, wd_ref, bd_ref, y_ref, wgu_bf, wd_bf):
    b = pl.program_id(0)

    @pl.when(b < nu_ref[0])
    def _():
        @pl.when((b == 0) | (be_ref[b] != be_ref[jnp.maximum(b - 1, 0)]))
        def _():
            wgu_bf[...] = wgu_ref[...].astype(bf16)
            wd_bf[...] = wd_ref[...].astype(bf16)

        ff = wd_bf.shape[0]
        x = _load_row_tiles(xs_ref, MOE_ROWS)
        gu = jnp.dot(x.astype(bf16), wgu_bf[...], preferred_element_type=f32) + bgu_ref[...]
        gate = jnp.minimum(gu[:, 0:ff], SWIGLU_LIMIT)
        up = jnp.clip(gu[:, ff:2 * ff], -SWIGLU_LIMIT, SWIGLU_LIMIT)
        act = gate * jax.nn.sigmoid(SWIGLU_ALPHA * gate) * (up + 1.0)
        _store_row_tiles(y_ref, 0, jnp.dot(act.astype(bf16), wd_bf[...], preferred_element_type=f32) + bd_ref[...])

    @pl.when(b >= nu_ref[0])
    def _():
        y_ref[...] = jnp.zeros_like(y_ref)


def _experts(block_e, n_used, xs, w_gate_up, b_gate_up, w_down, b_down):
    n_exp, d, ff2 = w_gate_up.shape
    ff = ff2 // 2
    block = (MOE_ROWS * ROW_CHUNKS, LANES)
    rows = lambda b, be, nu: (b, 0)
    per_e = lambda b, be, nu: (be[b], 0, 0)
    return pl.pallas_call(
        _expert_kernel,
        grid_spec=pltpu.PrefetchScalarGridSpec(
            num_scalar_prefetch=2, grid=(xs.shape[0] // block[0],),
            in_specs=[pl.BlockSpec(block, rows),
                      pl.BlockSpec((None, d, ff2), per_e), pl.BlockSpec((None, 1, ff2), per_e),
                      pl.BlockSpec((None, ff, d), per_e), pl.BlockSpec((None, 1, d), per_e)],
            out_specs=pl.BlockSpec(block, rows),
            scratch_shapes=[pltpu.VMEM((d, ff2), bf16), pltpu.VMEM((ff, d), bf16)]),
        out_shape=jax.ShapeDtypeStruct(xs.shape, f32),
        compiler_params=_cparams(("arbitrary",)),
        name="moe_experts",
    )(block_e, n_used, xs, w_gate_up, b_gate_up.reshape(n_exp, 1, ff2), w_down, b_down.reshape(n_exp, 1, d))


def _combine_kernel(pos_ref, y_ref, x1_ref, comb_ref, g2_ref, lng_ref, lnb_ref, o_ref, buf, sem, *, alpha, n_steps, step_fn):
    tc = x1_ref.shape[0]
    step = step_fn()

    def rows(s, slot, wait):
        base = s * tc

        def body(r2, c):
            for u in range(DMA_UNROLL):
                r = r2 * DMA_UNROLL + u
                for k in range(TOP_K):
                    p = 0 if wait else pos_ref[(base + r) * TOP_K + k]
                    cp = pltpu.make_async_copy(_row_tile(y_ref, p),
                                               _row_tile(buf, 0 if wait else r, slot * TOP_K + k), sem.at[slot])
                    if wait:
                        cp.wait()
                    else:
                        cp.start()
            return c

        lax.fori_loop(0, tc // DMA_UNROLL, body, 0)

    @pl.when(step == 0)
    def _():
        rows(0, 0, False)

    for sl in range(2):
        @pl.when((step + 1 < n_steps) & ((step + 1) % 2 == sl))
        def _(sl=sl):
            rows(step + 1, sl, False)

    slot = step % 2
    for sl in range(2):
        @pl.when(slot == sl)
        def _(sl=sl):
            rows(step, sl, True)

    comb = comb_ref[...]
    ffn = comb[:, 0:1] * _load_row_tiles(buf, tc, slot * TOP_K)
    for k in range(1, TOP_K):
        ffn = ffn + comb[:, k:k + 1] * _load_row_tiles(buf, tc, slot * TOP_K + k)
    o_ref[...] = _layer_norm(alpha * x1_ref[...] + g2_ref[...] * ffn, lng_ref[...], lnb_ref[...])


def _combine(pos, y, x1, comb, g2, lng, lnb, alpha, per_row_mods):
    d = x1.shape[-1]
    hbm = pl.BlockSpec(memory_space=pl.ANY)
    if per_row_mods:
        n = x1.shape[0]
        tc = min(COMBINE_TILE, n)
        grid = (n // tc,)
        row = lambda w: pl.BlockSpec((tc, w), lambda i, pos: (i, 0))
        mod = row(d)
        const = lambda shp: pl.BlockSpec(shp, lambda i, pos: (0,) * len(shp))
        step_fn = lambda: pl.program_id(0)
        n_steps = n // tc
        sem = ("arbitrary",)
        out_shape = jax.ShapeDtypeStruct((n, d), f32)
    else:
        b, s, _ = x1.shape
        tc = min(COMBINE_TILE, s)
        grid = (b, s // tc)
        row = lambda w: pl.BlockSpec((None, tc, w), lambda bi, i, pos: (bi, i, 0))
        mod = pl.BlockSpec((None, 1, d), lambda bi, i, pos: (bi, 0, 0))
        const = lambda shp: pl.BlockSpec(shp, lambda bi, i, pos: (0,) * len(shp))
        per_b = s // tc
        step_fn = lambda: pl.program_id(0) * per_b + pl.program_id(1)
        n_steps = b * per_b
        sem = ("arbitrary", "arbitrary")
        out_shape = jax.ShapeDtypeStruct((b, s, d), f32)
    return pl.pallas_call(
        functools.partial(_combine_kernel, alpha=alpha, n_steps=n_steps, step_fn=step_fn),
        grid_spec=pltpu.PrefetchScalarGridSpec(
            num_scalar_prefetch=1, grid=grid,
            in_specs=[hbm, row(d), row(LANES), mod, const((1, d)), const((1, d))],
            out_specs=row(d),
            scratch_shapes=[pltpu.VMEM((2 * TOP_K, tc * ROW_CHUNKS, LANES), f32), pltpu.SemaphoreType.DMA((2,))]),
        out_shape=out_shape,
        compiler_params=_cparams(sem),
        name="moe_combine_rows" if per_row_mods else "moe_combine_prompt",
    )(pos, y, x1, comb, g2, lng, lnb)


def _rope_tables(pos):
    n = pos.shape[0]
    inv_freq = ROPE_THETA ** (-jnp.arange(ROPE_HALF, dtype=f32) * 2.0 / ROPE_DIM)
    ang = pos[:, None] * inv_freq[None, :]
    cos, sin = jnp.cos(ang), jnp.sin(ang)
    rest = HEAD_DIM - ROPE_DIM
    z8, zr = jnp.zeros((n, ROPE_HALF), f32), jnp.zeros((n, rest), f32)
    c = jnp.concatenate([cos, cos, jnp.ones((n, rest), f32)], axis=1)
    s1 = jnp.concatenate([-sin, z8, zr], axis=1)
    s2 = jnp.concatenate([z8, sin, zr], axis=1)
    reps = LANES // HEAD_DIM
    return tuple(jnp.tile(a, (1, reps)) for a in (c, s1, s2))


def _rope_tables_t(pos):
    inv_freq = ROPE_THETA ** (-jnp.arange(ROPE_HALF, dtype=f32) * 2.0 / ROPE_DIM)
    ang = pos[:, None] * inv_freq[None, :]
    return jnp.cos(ang).T, jnp.sin(ang).T


def _pad_cols(a, width, value=0.0):
    return jnp.pad(a, ((0, 0), (0, width - a.shape[1])), constant_values=value)


def _layer(xp, xs, c_prompt, c_sample, cache_k, cache_v, cache_ki, state_conv, page_table, p, alpha):
    b, s, d = xp.shape
    db, t, _ = xs.shape
    ns = db * t
    assert t >= CONV_K - 1 and ns % SUBLANES == 0 and t <= SUBLANES

    n_c = b + db
    c_all = jnp.pad(jnp.concatenate([c_prompt, c_sample], axis=0), ((0, -n_c % SUBLANES), (0, 0)))
    ada = _ada(c_all, p["w_ada"], p["b_ada"])
    mods = [ada[:, j * d:(j + 1) * d] for j in range(6)]
    mp = [m[:b].reshape(b, 1, d) for m in mods]
    ms = [jnp.repeat(m[b:b + db], t, axis=0) for m in mods]

    sizes = (ATT_WIDTH, KV_WIDTH, KV_WIDTH, IDX_HEADS * IDX_DIM, IDX_DIM, IDX_HEADS, d // 2, d // 2, d // 2, d, d)
    cuts = [int(v) for v in np.cumsum(sizes)[:-1]]
    wq, wk, wv, wqi, wki, wwi, wxin, wbg, wcg, wga, wgb = jnp.split(p["w_in"], cuts, axis=1)
    w_proj = jnp.concatenate([wq, wk, wv, wqi, _pad_cols(wki, LANES), _pad_cols(wwi, LANES), wxin, wbg, wcg],
                             axis=1).astype(bf16)
    wg = jnp.concatenate([wga, wgb], axis=1).astype(bf16)
    post_w = (wg, p["w_branch_a"].astype(bf16), p["w_branch_b"].astype(bf16), p["w_o"].astype(bf16),
              _pad_cols(p["w_router"], LANES).astype(bf16), _pad_cols(p["b_router"][None, :], LANES, NEG_BIG),
              p["ln1_g"][None, :], p["ln1_b"][None, :])
    ln2 = (p["ln2_g"][None, :], p["ln2_b"][None, :])

    positions = jnp.arange(s, dtype=f32)
    w_t = jnp.concatenate([wk, wv, wki], axis=1).T.astype(bf16)
    q, qi, wi, kt, vt, kit, kt_b, vt_b, kit_b, conv_p, cst_p = _inproj_prompt(
        xp, mp[1], mp[0], _rope_tables(positions), _rope_tables_t(positions), w_proj, w_t, p["conv_w"])
    att_p = _attn_prompt(qi, wi, q, kit_b, kt_b, vt_b)

    past = page_table.shape[1] * PAGE_SIZE
    tabs_s = _rope_tables(jnp.tile(past + jnp.arange(t, dtype=f32), db))
    zeros = jnp.zeros((db, t, d // 2), f32)
    p1 = zeros.at[:, 0].set(state_conv[:, 1]).reshape(ns, d // 2)
    p2 = zeros.at[:, 0].set(state_conv[:, 0]).at[:, 1].set(state_conv[:, 1]).reshape(ns, d // 2)
    qs, ks, vs, qis, kis, wis, conv_s, u_s = _inproj_sample(xs.reshape(ns, d), ms[1], ms[0], tabs_s, w_proj,
                                                            p["conv_w"], p1, p2, t)
    pad_t = lambda a, rows: jnp.pad(a.reshape(db, t, -1), ((0, 0), (0, rows - t), (0, 0)))
    head_rows = lambda a: pad_t(a, SUBLANES).reshape(db, SUBLANES, -1, HEAD_DIM).transpose(0, 2, 1, 3).reshape(
        db, -1, HEAD_DIM)
    new_cols = lambda a: jnp.pad(a.reshape(db, t, -1).transpose(0, 2, 1), ((0, 0), (0, 0), (0, LANES - t)))
    n_pool = cache_k.shape[0]
    feature_major = lambda c: c.transpose(0, 2, 3, 1).reshape(n_pool, KV_WIDTH, PAGE_SIZE)
    bias_s = _select_sample(page_table, head_rows(qis), pad_t(wis, SUBLANES), new_cols(kis),
                            cache_ki.transpose(0, 2, 1), t)
    att_s = _attn_sample(page_table, head_rows(qs), bias_s, new_cols(ks), new_cols(vs),
                         feature_major(cache_k), feature_major(cache_v))
    att_s = att_s.reshape(db, N_HEADS, SUBLANES, HEAD_DIM)[:, :, :t].transpose(0, 2, 1, 3).reshape(ns, ATT_WIDTH)

    cnt0 = jnp.zeros((1, LANES), f32)
    x1p, h2p, tope_p, comb_p, rank_p, cnt1 = _post(xp, att_p, conv_p, (mp[1], mp[0], mp[2], mp[4], mp[3]),
                                                   post_w, cnt0, alpha, False)
    x1s, h2s, tope_s, comb_s, rank_s, cnt2 = _post(xs.reshape(ns, d), att_s, conv_s, (ms[1], ms[0], ms[2], ms[4], ms[3]),
                                                   post_w, cnt1, alpha, True)

    n_p = b * s
    counts = cnt2[0, :N_EXPERTS].astype(i32)
    padded = (counts + MOE_ROWS - 1) // MOE_ROWS * MOE_ROWS
    pend = jnp.cumsum(padded).astype(i32)
    pstart = pend - padded
    tope = jnp.concatenate([tope_p.reshape(n_p, LANES)[:, :TOP_K], tope_s[:, :TOP_K]], axis=0)
    rank = jnp.concatenate([rank_p.reshape(n_p, LANES)[:, :TOP_K], rank_s[:, :TOP_K]], axis=0)
    pos = (pstart[tope] + rank).astype(i32)
    n_blocks = ((n_p + ns) * TOP_K + N_EXPERTS * (MOE_ROWS - 1)) // MOE_ROWS
    n_used = pend[-1:] // MOE_ROWS
    blk = jnp.minimum(jnp.arange(n_blocks, dtype=i32), n_used[0] - 1)
    block_e = jnp.minimum(jnp.sum(pend[None, :] <= (blk * MOE_ROWS)[:, None], axis=1), N_EXPERTS - 1).astype(i32)
    pos_p, pos_s = pos[:n_p].reshape(-1), pos[n_p:].reshape(-1)

    rows = _dispatch(pos.reshape(-1), pend, h2p.reshape(n_p * ROW_CHUNKS, LANES), h2s, n_blocks)
    y = _experts(block_e, n_used, rows, p["w_gate_up"], p["b_gate_up"], p["w_down"], p["b_down"])
    yp = _combine(pos_p, y, x1p, comb_p, mp[5], *ln2, alpha, False)
    ys = _combine(pos_s, y, x1s, comb_s, ms[5], *ln2, alpha, True).reshape(db, t, d)

    token_major = lambda a: a.reshape(b, N_KV_HEADS, HEAD_DIM, s).transpose(0, 3, 1, 2)
    new_p = (token_major(kt), token_major(vt), kit.transpose(0, 2, 1), cst_p)
    new_s = (ks.reshape(db, t, N_KV_HEADS, HEAD_DIM), vs.reshape(db, t, N_KV_HEADS, HEAD_DIM),
             kis.reshape(db, t, IDX_DIM), u_s.reshape(db, t, d // 2)[:, t - (CONV_K - 1):])
    return yp, ys, new_p, new_s


def kernel(x_prompt, x_sample, cache_k, cache_v, cache_idx_k, state_conv, page_table, c_prompt, c_sample,
           w_ada, b_ada, w_in, conv_w, w_branch_a, w_branch_b, w_o, ln1_g, ln1_b, ln2_g, ln2_b,
           w_router, b_router, w_gate_up, b_gate_up, w_down, b_down):
    depth = w_ada.shape[0]
    alpha = float((2 * depth) ** 0.25)
    xp, xs = x_prompt, x_sample
    acc_p, acc_s = [], []
    for l in range(depth):
        p = {"w_ada": w_ada[l], "b_ada": b_ada[l], "w_in": w_in[l], "conv_w": conv_w[l],
             "w_branch_a": w_branch_a[l], "w_branch_b": w_branch_b[l], "w_o": w_o[l],
             "ln1_g": ln1_g[l], "ln1_b": ln1_b[l], "ln2_g": ln2_g[l], "ln2_b": ln2_b[l],
             "w_router": w_router[l], "b_router": b_router[l], "w_gate_up": w_gate_up[l],
             "b_gate_up": b_gate_up[l], "w_down": w_down[l], "b_down": b_down[l]}
        xp, xs, new_p, new_s = _layer(xp, xs, c_prompt, c_sample, cache_k[l], cache_v[l], cache_idx_k[l],
                                      state_conv[l], page_table, p, alpha)
        acc_p.append(new_p)
        acc_s.append(new_s)
    stack = lambda acc, j: jnp.stack([a[j] for a in acc])
    return (xp, xs, stack(acc_p, 0), stack(acc_p, 1), stack(acc_p, 2), stack(acc_p, 3),
            stack(acc_s, 0), stack(acc_s, 1), stack(acc_s, 2), stack(acc_s, 3))
```
